```python
import jax, jax.numpy as jnp
from jax import lax
import numpy as np

D_MODEL = 1024
BATCH = 16
SEQ = 2048
DEPTH = 4

CTX_LEN = 256
GRID_W = 64

ML_W = D_MODEL // 2
ML_HEADS = 4
ML_DH = ML_W // ML_HEADS
ML_CHUNK = 128
CV_W = D_MODEL // 4
CV_GROUPS = 4
CV_WIDTH = 3
GM_W = D_MODEL // 4
GM_GROUPS = 4
GM_CHUNK = 128
N_GATES = 4 * ML_HEADS
D_FF = ((8 * D_MODEL // 3 + 255) // 256) * 256
EPS = 1e-6

OFF_K = 0
OFF_V = OFF_K + ML_W
OFF_G = OFF_V + ML_W
OFF_Q = OFF_G + N_GATES
OFF_O = OFF_Q + ML_W
OFF_CV = OFF_O + ML_W
OFF_GM = OFF_CV + 3 * CV_W
D_IN = OFF_GM + 2 * GM_W

kernel_name = "hybrid_mlstm_conv_gmlp_dit"


def _rmsnorm(x, g):
    xf = x.astype(jnp.float32)
    y = xf * lax.rsqrt(jnp.mean(xf * xf, axis=-1, keepdims=True) + EPS) * g.astype(jnp.float32)
    return y.astype(x.dtype)


def _modulation(c_vec, w_ada, b_ada):
    return jnp.split(jax.nn.silu(c_vec) @ w_ada + b_ada, 6, axis=-1)


def _modulate(h, shift, scale):
    return h * (1 + scale) + shift


def _swiglu(h, w1, w3, w2):
    return (jax.nn.silu(h @ w1) * (h @ w3)) @ w2


def _flip(a):
    return jnp.flip(a, axis=2)


def _heads(a):
    bsz, t, _ = a.shape
    return a.reshape(bsz, t, ML_HEADS, ML_DH).transpose(0, 2, 1, 3).astype(jnp.float32)


def _mlstm_kvg(p, b_gates):
    k = _heads(p[..., OFF_K:OFF_K + ML_W]) * (ML_DH ** -0.5)
    v = _heads(p[..., OFF_V:OFF_V + ML_W])
    bsz, t, _ = p.shape
    g = p[..., OFF_G:OFF_G + N_GATES].astype(jnp.float32) + b_gates.astype(jnp.float32)
    g = g.reshape(bsz, t, 4, ML_HEADS).transpose(2, 0, 3, 1)
    li_f, lf_f = g[0], jax.nn.log_sigmoid(g[1])
    li_b, lf_b = g[2], jax.nn.log_sigmoid(g[3])
    return k, v, (li_f, lf_f, li_b, lf_b)


def _mlstm_qo(p):
    q = _heads(p[..., OFF_Q:OFF_Q + ML_W])
    o = jax.nn.sigmoid(p[..., OFF_O:OFF_O + ML_W].astype(jnp.float32))
    return q, o


def _zero_state(bsz):
    return (jnp.zeros((bsz, ML_HEADS, ML_DH, ML_DH), jnp.float32),
            jnp.zeros((bsz, ML_HEADS, ML_DH), jnp.float32),
            jnp.zeros((bsz, ML_HEADS), jnp.float32))


def _mlstm_chunks(q, k, v, li, lf, state):
    bsz, nh, t, dh = q.shape
    nc = t // ML_CHUNK

    def chunk(a):
        a = a.reshape(bsz, nh, nc, ML_CHUNK, *a.shape[3:])
        return jnp.moveaxis(a, 2, 0)

    lower = jnp.tril(jnp.ones((ML_CHUNK, ML_CHUNK), bool))

    def step(carry, inp):
        c_st, n_st, m_st = carry
        qc, kc, vc, lic, lfc = inp
        b = jnp.cumsum(lfc, axis=-1)
        dmat = b[..., :, None] - b[..., None, :] + lic[..., None, :]
        dmat = jnp.where(lower, dmat, -jnp.inf)
        m_inter = b + m_st[..., None]
        m_t = jnp.maximum(m_inter, jnp.max(dmat, axis=-1))
        s = jnp.einsum('bhtd,bhsd->bhts', qc, kc) * jnp.exp(dmat - m_t[..., None])
        inter = jnp.exp(m_inter - m_t)
        num = (jnp.einsum('bhts,bhsd->bhtd', s, vc)
               + inter[..., None] * jnp.einsum('bhtk,bhkv->bhtv', qc, c_st))
        den = jnp.sum(s, axis=-1) + inter * jnp.einsum('bhtk,bhk->bht', qc, n_st)
        h = num / jnp.maximum(jnp.abs(den), jnp.exp(-m_t))[..., None]
        m_new = m_t[..., -1]
        decay = jnp.exp(m_inter[..., -1] - m_new)
        ws = jnp.exp(b[..., -1:] - b + lic - m_new[..., None])
        c_new = decay[..., None, None] * c_st + jnp.einsum('bhs,bhsk,bhsv->bhkv', ws, kc, vc)
        n_new = decay[..., None] * n_st + jnp.einsum('bhs,bhsk->bhk', ws, kc)
        return (c_new, n_new, m_new), h

    state, h = lax.scan(step, state, (chunk(q), chunk(k), chunk(v), chunk(li), chunk(lf)))
    h = jnp.moveaxis(h, 0, 2).reshape(bsz, nh, t, dh)
    return h, state


def _mlstm_final_state(k, v, li, lf):
    b = jnp.cumsum(lf, axis=-1)
    b_end = b[..., -1:]
    logw = b_end - b + li
    m = jnp.maximum(b_end[..., 0], jnp.max(logw, axis=-1))
    w = jnp.exp(logw - m[..., None])
    c_st = jnp.einsum('bhs,bhsk,bhsv->bhkv', w, k, v)
    n_st = jnp.einsum('bhs,bhsk->bhk', w, k)
    return (c_st, n_st, m)


def _mlstm_out(h, o, ml_norm, dtype):
    hn = h * lax.rsqrt(jnp.mean(h * h, axis=-1, keepdims=True) + EPS) * ml_norm[:, None, :].astype(jnp.float32)
    bsz, _, t, _ = h.shape
    hn = hn.transpose(0, 2, 1, 3).reshape(bsz, t, ML_W)
    return (o * hn).astype(dtype)


def _mlstm_bidir(p, b_gates, ml_norm, st_f, st_b):
    q, o = _mlstm_qo(p)
    k, v, (li_f, lf_f, li_b, lf_b) = _mlstm_kvg(p, b_gates)
    h_f, st_f_out = _mlstm_chunks(q, k, v, li_f, lf_f, st_f)
    h_b, st_b_out = _mlstm_chunks(_flip(q), _flip(k), _flip(v), _flip(li_b), _flip(lf_b), st_b)
    return _mlstm_out(h_f + _flip(h_b), o, ml_norm, p.dtype), st_f_out, st_b_out


def _short_conv(p, w_conv, grid):
    bsz, t, _ = p.shape
    gb = p[..., OFF_CV:OFF_CV + CV_W]
    gc = p[..., OFF_CV + CV_W:OFF_CV + 2 * CV_W]
    hin = p[..., OFF_CV + 2 * CV_W:OFF_CV + 3 * CV_W]
    z = gc * hin
    if grid:
        rows = t // GRID_W
        z = z.reshape(bsz, rows, GRID_W, CV_W)
    zp = jnp.pad(z, [(0, 0)] * (z.ndim - 2) + [(1, 1), (0, 0)])
    y = w_conv[0] * zp[..., :-2, :] + w_conv[1] * zp[..., 1:-1, :] + w_conv[2] * zp[..., 2:, :]
    return gb * y.reshape(bsz, t, CV_W)


def _chunk_mlp(p, gm_norm, w_s, b_s):
    bsz, t, _ = p.shape
    u = p[..., OFF_GM:OFF_GM + GM_W]
    v = _rmsnorm(p[..., OFF_GM + GM_W:OFF_GM + 2 * GM_W], gm_norm)
    v = v.reshape(bsz, t // GM_CHUNK, GM_CHUNK, GM_GROUPS, GM_W // GM_GROUPS)
    z = jnp.einsum('gts,bnsgc->bntgc', w_s, v) + b_s.T[:, :, None]
    return u * z.reshape(bsz, t, GM_W)


def setup_inputs(seed: int = 0) -> dict:
    key = jax.random.key(seed)
    ks = jax.random.split(key, 24)
    f32 = jnp.float32
    nrm = lambda k, shape, s: jax.random.normal(k, shape, f32) * s
    fbias = jnp.linspace(3.0, 6.0, ML_HEADS, dtype=f32)
    zb = jnp.zeros((ML_HEADS,), f32)
    gate_base = jnp.concatenate([zb, fbias, zb, fbias])
    return {
        "x": nrm(ks[0], (BATCH, SEQ, D_MODEL), 1.0),
        "c": nrm(ks[1], (BATCH, D_MODEL), 1.0),
        "ctx": nrm(ks[2], (BATCH, CTX_LEN, D_MODEL), 1.0),
        "c_ctx": nrm(ks[3], (D_MODEL,), 1.0),
        "w_ada": nrm(ks[4], (DEPTH, D_MODEL, 6 * D_MODEL), 0.5 * D_MODEL ** -0.5),
        "b_ada": nrm(ks[5], (DEPTH, 6 * D_MODEL), 0.02),
        "norm1": 1.0 + nrm(ks[6], (DEPTH, D_MODEL), 0.05),
        "norm2": 1.0 + nrm(ks[7], (DEPTH, D_MODEL), 0.05),
        "w_in": nrm(ks[8], (DEPTH, D_MODEL, D_IN), D_MODEL ** -0.5),
        "b_gates": gate_base + nrm(ks[9], (DEPTH, N_GATES), 0.1),
        "ml_norm": 1.0 + nrm(ks[10], (DEPTH, ML_HEADS, ML_DH), 0.05),
        "conv_w": nrm(ks[11], (DEPTH, CV_WIDTH, CV_W), CV_WIDTH ** -0.5),
        "gm_norm": 1.0 + nrm(ks[12], (DEPTH, GM_W), 0.05),
        "gm_ws": nrm(ks[13], (DEPTH, GM_GROUPS, GM_CHUNK, GM_CHUNK), GM_CHUNK ** -0.5),
        "gm_bs": 1.0 + nrm(ks[14], (DEPTH, GM_GROUPS, GM_CHUNK), 0.1),
        "w_out": nrm(ks[15], (DEPTH, D_MODEL, D_MODEL), D_MODEL ** -0.5),
        "w1": nrm(ks[16], (DEPTH, D_MODEL, D_FF), D_MODEL ** -0.5),
        "w3": nrm(ks[17], (DEPTH, D_MODEL, D_FF), D_MODEL ** -0.5),
        "w2": nrm(ks[18], (DEPTH, D_FF, D_MODEL), D_FF ** -0.5),
        "norm_f": 1.0 + nrm(ks[19], (D_MODEL,), 0.05),
    }


def reference(x, c, ctx, c_ctx, w_ada, b_ada, norm1, norm2, w_in, b_gates, ml_norm,
              conv_w, gm_norm, gm_ws, gm_bs, w_out, w1, w3, w2, norm_f):
    bsz = x.shape[0]
    for l in range(DEPTH):
        last = l == DEPTH - 1
        sh1, sc1, g1, sh2, sc2, g2 = [m[:, None, :] for m in _modulation(c, w_ada[l], b_ada[l])]
        csh1, csc1, cg1, csh2, csc2, cg2 = _modulation(c_ctx, w_ada[l], b_ada[l])
        hc = _modulate(_rmsnorm(ctx, norm1[l]), csh1, csc1)
        if last:
            pc = hc @ w_in[l][:, :OFF_Q]
            kc, vc, (lic_f, lfc_f, lic_b, lfc_b) = _mlstm_kvg(pc, b_gates[l])
            st_f = _mlstm_final_state(kc, vc, lic_f, lfc_f)
            st_b = _mlstm_final_state(_flip(kc), _flip(vc), _flip(lic_b), _flip(lfc_b))
        else:
            pc = hc @ w_in[l]
            zero = _zero_state(bsz)
            ml_c, st_f, st_b = _mlstm_bidir(pc, b_gates[l], ml_norm[l], zero, zero)
            mix_c = jnp.concatenate([ml_c,
                                     _short_conv(pc, conv_w[l], False),
                                     _chunk_mlp(pc, gm_norm[l], gm_ws[l], gm_bs[l])], axis=-1)
            ctx = ctx + cg1 * (mix_c @ w_out[l])
            hc2 = _modulate(_rmsnorm(ctx, norm2[l]), csh2, csc2)
            ctx = ctx + cg2 * _swiglu(hc2, w1[l], w3[l], w2[l])
        hx = _modulate(_rmsnorm(x, norm1[l]), sh1, sc1)
        px = hx @ w_in[l]
        ml_x, _, _ = _mlstm_bidir(px, b_gates[l], ml_norm[l], st_f, st_b)
        mix_x = jnp.concatenate([ml_x,
                                 _short_conv(px, conv_w[l], True),
                                 _chunk_mlp(px, gm_norm[l], gm_ws[l], gm_bs[l])], axis=-1)
        x = x + g1 * (mix_x @ w_out[l])
        hx2 = _modulate(_rmsnorm(x, norm2[l]), sh2, sc2)
        x = x + g2 * _swiglu(hx2, w1[l], w3[l], w2[l])
    return _rmsnorm(x, norm_f)
```

```python
import functools
import math

import jax
import jax.numpy as jnp
from jax import lax
from jax.experimental import pallas as pl
from jax.experimental.pallas import tpu as pltpu

D_MODEL = 1024
GRID_W = 64
ML_W = D_MODEL // 2
ML_HEADS = 4
ML_DH = ML_W // ML_HEADS
CHUNK = 128
CV_W = D_MODEL // 4
GM_W = D_MODEL // 4
GM_GROUPS = 4
N_GATES = 4 * ML_HEADS
D_FF = 2816
EPS = 1e-6
LANES = 128
N_DIRHEAD = 2 * ML_HEADS

_OFF_K = 0
_OFF_V = _OFF_K + ML_W
_OFF_G = _OFF_V + ML_W
_OFF_Q = _OFF_G + N_GATES
_OFF_O = _OFF_Q + ML_W
_OFF_CV = _OFF_O + ML_W
_OFF_GM = _OFF_CV + 3 * CV_W
_D_IN = _OFF_GM + 2 * GM_W

P_ML_W = 4 * ML_W
P_LOC_W = 3 * CV_W + 2 * GM_W
MOD_ROWS = 24
TILE_ROWS = 256

_VMEM_LIMIT = 56 * 1024 * 1024

_BF16 = jnp.bfloat16
_F32 = jnp.float32


def _dot(a, b):
    return jnp.dot(a, b, preferred_element_type=_F32)


def _rms_scale(x):
    return lax.rsqrt(jnp.mean(x * x, axis=-1, keepdims=True) + EPS)


def _ada_kernel(c_ref, w_ref, b_ref, o_ref):
    c = c_ref[...]
    s = (c * jax.nn.sigmoid(c)).astype(_BF16)
    o_ref[...] = _dot(s, w_ref[...].astype(_BF16)) + b_ref[...]


def _ada_table(cc, w_ada, b_ada):
    depth, _, n = w_ada.shape
    tn = 1536
    return pl.pallas_call(
        _ada_kernel,
        grid=(depth, n // tn),
        in_specs=[
            pl.BlockSpec((MOD_ROWS, D_MODEL), lambda l, j: (0, 0)),
            pl.BlockSpec((None, D_MODEL, tn), lambda l, j: (l, 0, j)),
            pl.BlockSpec((None, 1, tn), lambda l, j: (l, 0, j)),
        ],
        out_specs=pl.BlockSpec((None, MOD_ROWS, tn), lambda l, j: (l, 0, j)),
        out_shape=jax.ShapeDtypeStruct((depth, MOD_ROWS, n), _F32),
        compiler_params=pltpu.CompilerParams(vmem_limit_bytes=_VMEM_LIMIT),
        name="ada_table",
    )(cc, w_ada, b_ada.reshape(depth, 1, n))


def _mod_row(ctx_tile, batch_rows):
    return jnp.where(pl.program_id(1) == ctx_tile, batch_rows, pl.program_id(0))


def _in_proj_kernel(x_ref, sh_ref, sc_ref, nw_ref, w_ref, wg_ref, pml_ref, ploc_ref, g_ref,
                    *, ctx_tile, batch_rows):
    row = _mod_row(ctx_tile, batch_rows)
    x = x_ref[...]
    xn = x * _rms_scale(x) * nw_ref[...]
    hx = xn * (1.0 + sc_ref[pl.ds(row, 1), :]) + sh_ref[pl.ds(row, 1), :]
    hb = hx.astype(_BF16)
    p = _dot(hb, w_ref[...])
    pml_ref[...] = p[:, :P_ML_W].astype(_BF16)
    ploc_ref[...] = p[:, P_ML_W:].astype(_BF16)
    g_ref[...] = _dot(hb, wg_ref[...])


def _in_proj(xs, mods, layer, norm_w, w_main, w_gate, *, tile_lo, n_tiles, ctx_tile):
    bsz, t, _ = xs.shape
    tm = TILE_ROWS
    kern = functools.partial(_in_proj_kernel, ctx_tile=ctx_tile - tile_lo, batch_rows=bsz)
    row_map = lambda b, i: (b, i + tile_lo, 0)
    return pl.pallas_call(
        kern,
        grid=(bsz, n_tiles),
        in_specs=[
            pl.BlockSpec((None, tm, D_MODEL), row_map),
            pl.BlockSpec((None, MOD_ROWS, D_MODEL), lambda b, i: (layer, 0, 0)),
            pl.BlockSpec((None, MOD_ROWS, D_MODEL), lambda b, i: (layer, 0, 1)),
            pl.BlockSpec((None, 1, D_MODEL), lambda b, i: (layer, 0, 0)),
            pl.BlockSpec((None, D_MODEL, P_ML_W + P_LOC_W), lambda b, i: (layer, 0, 0)),
            pl.BlockSpec((None, D_MODEL, LANES), lambda b, i: (layer, 0, 0)),
        ],
        out_specs=[
            pl.BlockSpec((None, tm, P_ML_W), row_map),
            pl.BlockSpec((None, tm, P_LOC_W), row_map),
            pl.BlockSpec((None, tm, LANES), row_map),
        ],
        out_shape=[
            jax.ShapeDtypeStruct((bsz, t, P_ML_W), _BF16),
            jax.ShapeDtypeStruct((bsz, t, P_LOC_W), _BF16),
            jax.ShapeDtypeStruct((bsz, t, LANES), _F32),
        ],
        compiler_params=pltpu.CompilerParams(vmem_limit_bytes=_VMEM_LIMIT),
        name="in_proj",
    )(xs, mods, mods, norm_w, w_main, w_gate)


def _directional_scan(x, combine, fill):
    row = lax.broadcasted_iota(jnp.int32, x.shape, 0)
    lane = lax.broadcasted_iota(jnp.int32, x.shape, 1)
    fwd = row < ML_HEADS
    n = x.shape[1]
    s = 1
    while s < n:
        prev = jnp.where(lane >= s, pltpu.roll(x, s, 1), fill)
        nxt = jnp.where(lane < n - s, pltpu.roll(x, n - s, 1), fill)
        x = combine(x, jnp.where(fwd, prev, nxt))
        s *= 2
    return x


def _mlstm_kernel(p_ref, g_ref, gb_ref, mln_ref, o_ref,
                  r_s, cm_s, b_s, st_s, cn_s, hf_s, hb_s, *, n_chunks, n_ctx_chunks, ctx_out):
    L = CHUNK
    n_lat = n_chunks - n_ctx_chunks
    k_scale = ML_DH ** -0.5
    neg_inf = -jnp.inf

    def prep(c, carry):
        g = g_ref[pl.ds(pl.multiple_of(c * L, L), L), :] + gb_ref[...]
        gt = g.T
        li = gt[0:N_DIRHEAD]
        lf = jax.nn.log_sigmoid(gt[N_DIRHEAD:2 * N_DIRHEAD])
        b = _directional_scan(lf, jnp.add, 0.0)
        r = li - b
        cm = _directional_scan(r, jnp.maximum, neg_inf)
        r_s[c] = r
        cm_s[c] = cm
        b_s[c] = b
        st_s[c, 0] = jnp.broadcast_to(jnp.sum(lf, axis=1, keepdims=True), (N_DIRHEAD, L))
        st_s[c, 1] = jnp.broadcast_to(jnp.max(r, axis=1, keepdims=True), (N_DIRHEAD, L))
        return carry

    lax.fori_loop(0, n_chunks, prep, 0)

    cn_s[...] = jnp.zeros_like(cn_s)
    row8 = lax.broadcasted_iota(jnp.int32, (N_DIRHEAD, L), 0)
    fwd_rows = row8 < ML_HEADS
    t_idx = lax.broadcasted_iota(jnp.int32, (L, L), 0)
    s_idx = lax.broadcasted_iota(jnp.int32, (L, L), 1)
    causal = (s_idx <= t_idx, s_idx >= t_idx)
    ones_ext = jnp.ones((L, ML_DH), _BF16)
    pad_rows = jnp.zeros((LANES - 5 * N_DIRHEAD, L), _F32)

    def make_step(with_out):
        def step(i, m_prev):
            cf = jnp.where(i < n_ctx_chunks, n_lat + i, i - n_ctx_chunks)
            cb = n_chunks - 1 - i
            pick = lambda ref, *idx: jnp.where(fwd_rows, ref[(cf,) + idx], ref[(cb,) + idx])
            r8 = pick(r_s)
            cm8 = pick(cm_s)
            b8 = pick(b_s)
            bsum = pick(st_s, 0)
            rmax = pick(st_s, 1)
            a8 = jnp.maximum(m_prev, cm8)
            f_intra = k_scale * jnp.exp(cm8 - a8)
            f_inter = jnp.exp(m_prev - a8)
            e_negm = jnp.exp(-(b8 + a8))
            f_kw = k_scale * jnp.exp(r8 - rmax)
            a_last = jnp.maximum(m_prev, rmax)
            f_upd = jnp.exp(rmax - a_last)
            decay = jnp.exp(m_prev - a_last)
            m_new = bsum + a_last
            cols = jnp.concatenate([f_intra, f_inter, e_negm, f_kw, cm8, pad_rows], axis=0).T

            for d in range(2):
                c = cf if d == 0 else cb
                rows = pl.ds(pl.multiple_of(c * L, L), L)
                hd_s = hf_s if d == 0 else hb_s
                for h in range(ML_HEADS):
                    j = d * ML_HEADS + h
                    col = lambda q: cols[:, q * N_DIRHEAD + j:q * N_DIRHEAD + j + 1]
                    k = p_ref[rows, h * ML_DH:(h + 1) * ML_DH]
                    v = p_ref[rows, ML_W + h * ML_DH:ML_W + (h + 1) * ML_DH]
                    vext = jnp.concatenate([v, ones_ext], axis=1)
                    if with_out:
                        q = p_ref[rows, 2 * ML_W + h * ML_DH:2 * ML_W + (h + 1) * ML_DH]
                        qk = lax.dot_general(q, k, (((1,), (1,)), ((), ())), preferred_element_type=_F32)
                        logw = jnp.where(causal[d], r8[j:j + 1, :] - col(4), neg_inf)
                        s0 = (qk * jnp.exp(logw)).astype(_BF16)
                        intra = _dot(s0, vext)
                        inter = _dot(q, cn_s[j].astype(_BF16))
                        z = col(0) * intra + col(1) * inter
                        den = jnp.maximum(jnp.abs(z[:, ML_DH:]), col(2))
                        hd_s[rows, h * ML_DH:(h + 1) * ML_DH] = z[:, :ML_DH] / den
                    kw = (col(3) * k.astype(_F32)).astype(_BF16)
                    upd = lax.dot_general(kw, vext, (((0,), (0,)), ((), ())), preferred_element_type=_F32)
                    cn_s[j] = decay[j:j + 1, 0:1] * cn_s[j] + f_upd[j:j + 1, 0:1] * upd
            return m_new
        return step

    m0 = jnp.zeros((N_DIRHEAD, L), _F32)
    m1 = lax.fori_loop(0, n_ctx_chunks, make_step(ctx_out), m0)
    lax.fori_loop(n_ctx_chunks, n_chunks, make_step(True), m1)

    def finish(c, carry):
        rows = pl.ds(pl.multiple_of(c * L, L), L)
        for h in range(ML_HEADS):
            cs = slice(h * ML_DH, (h + 1) * ML_DH)
            hh = hf_s[rows, cs] + hb_s[rows, cs]
            hn = hh * _rms_scale(hh) * mln_ref[:, cs]
            o = jax.nn.sigmoid(p_ref[rows, 3 * ML_W + h * ML_DH:3 * ML_W + (h + 1) * ML_DH].astype(_F32))
            o_ref[rows, cs] = (o * hn).astype(_BF16)
        return carry

    lax.fori_loop(0, n_chunks if ctx_out else n_lat, finish, 0)
    if not ctx_out:
        o_ref[n_lat * L:, :] = jnp.zeros((n_ctx_chunks * L, ML_W), _BF16)


def _mlstm(p_ml, gates, gate_bias, ml_norm, *, n_ctx_chunks, ctx_out):
    bsz, t, _ = p_ml.shape
    n_chunks = t // CHUNK
    kern = functools.partial(_mlstm_kernel, n_chunks=n_chunks, n_ctx_chunks=n_ctx_chunks, ctx_out=ctx_out)
    return pl.pallas_call(
        kern,
        grid=(bsz,),
        in_specs=[
            pl.BlockSpec((None, t, P_ML_W), lambda b: (b, 0, 0)),
            pl.BlockSpec((None, t, LANES), lambda b: (b, 0, 0)),
            pl.BlockSpec((1, LANES), lambda b: (0, 0)),
            pl.BlockSpec((1, ML_W), lambda b: (0, 0)),
        ],
        out_specs=pl.BlockSpec((None, t, ML_W), lambda b: (b, 0, 0)),
        out_shape=jax.ShapeDtypeStruct((bsz, t, ML_W), _BF16),
        scratch_shapes=[
            pltpu.VMEM((n_chunks, N_DIRHEAD, CHUNK), _F32),
            pltpu.VMEM((n_chunks, N_DIRHEAD, CHUNK), _F32),
            pltpu.VMEM((n_chunks, N_DIRHEAD, CHUNK), _F32),
            pltpu.VMEM((n_chunks, 2, N_DIRHEAD, CHUNK), _F32),
            pltpu.VMEM((N_DIRHEAD, ML_DH, 2 * ML_DH), _F32),
            pltpu.VMEM((t, ML_W), _F32),
            pltpu.VMEM((t, ML_W), _F32),
        ],
        compiler_params=pltpu.CompilerParams(vmem_limit_bytes=_VMEM_LIMIT),
        name="mlstm",
    )(p_ml, gates, gate_bias, ml_norm)


def _out_ffn_kernel(x_ref, ml_ref, ploc_ref, g1_ref, sh2_ref, sc2_ref, g2_ref, n2_ref, cw_ref, gmn_ref,
                    gmw_ref, gmb_ref, wo_ref, w1_ref, w3_ref, w2_ref, nf_ref, o_ref,
                    *, ctx_tile, batch_rows, final_norm):
    tm = x_ref.shape[0]
    row = _mod_row(ctx_tile, batch_rows)
    is_ctx = pl.program_id(1) == ctx_tile

    gate_b = ploc_ref[:, 0:CV_W].astype(_F32)
    z = ploc_ref[:, CV_W:2 * CV_W].astype(_F32) * ploc_ref[:, 2 * CV_W:3 * CV_W].astype(_F32)
    t_idx = lax.broadcasted_iota(jnp.int32, (tm, CV_W), 0)
    period = jnp.where(is_ctx, tm, GRID_W)
    pos = t_idx & (period - 1)
    z_prev = jnp.where(pos == 0, 0.0, pltpu.roll(z, 1, 0))
    z_next = jnp.where(pos == period - 1, 0.0, pltpu.roll(z, tm - 1, 0))
    conv = gate_b * (cw_ref[0:1, :] * z_prev + cw_ref[1:2, :] * z + cw_ref[2:3, :] * z_next)

    u = ploc_ref[:, 3 * CV_W:3 * CV_W + GM_W].astype(_F32)
    v = ploc_ref[:, 3 * CV_W + GM_W:3 * CV_W + 2 * GM_W].astype(_F32)
    vn = (v * _rms_scale(v) * gmn_ref[...]).astype(_BF16)
    group = lax.broadcasted_iota(jnp.int32, (CHUNK, GM_W), 1) // (GM_W // GM_GROUPS)
    zs = []
    for n in range(tm // CHUNK):
        vc = vn[n * CHUNK:(n + 1) * CHUNK]
        zc = gmb_ref[...]
        for g in range(GM_GROUPS):
            zc = zc + jnp.where(group == g, _dot(gmw_ref[g], vc), 0.0)
        zs.append(zc)
    gm = u * jnp.concatenate(zs, axis=0)

    mix = jnp.concatenate([ml_ref[...], conv.astype(_BF16), gm.astype(_BF16)], axis=1)
    x1 = x_ref[...] + g1_ref[pl.ds(row, 1), :] * _dot(mix, wo_ref[...])

    hx2 = x1 * _rms_scale(x1) * n2_ref[...]
    hx2 = (hx2 * (1.0 + sc2_ref[pl.ds(row, 1), :]) + sh2_ref[pl.ds(row, 1), :]).astype(_BF16)
    a = _dot(hx2, w1_ref[...])
    act = (a * jax.nn.sigmoid(a) * _dot(hx2, w3_ref[...])).astype(_BF16)
    x2 = x1 + g2_ref[pl.ds(row, 1), :] * _dot(act, w2_ref[...])
    if final_norm:
        x2 = x2 * _rms_scale(x2) * nf_ref[...]
    o_ref[...] = x2


def _out_ffn(xs, ml, p_loc, mods, layer, norm2, conv_w, gm_norm, gm_w, gm_b, w_out, w1, w3, w2, norm_f,
             *, n_tiles, ctx_tile, final_norm):
    bsz, t, _ = xs.shape
    tm = TILE_ROWS
    kern = functools.partial(_out_ffn_kernel, ctx_tile=ctx_tile, batch_rows=bsz, final_norm=final_norm)
    row_map = lambda b, i: (b, i, 0)
    mod_spec = lambda k: pl.BlockSpec((None, MOD_ROWS, D_MODEL), lambda b, i: (layer, 0, k))
    lay = lambda *shape: pl.BlockSpec((None,) + shape, lambda b, i: (layer,) + (0,) * len(shape))
    return pl.pallas_call(
        kern,
        grid=(bsz, n_tiles),
        in_specs=[
            pl.BlockSpec((None, tm, D_MODEL), row_map),
            pl.BlockSpec((None, tm, ML_W), row_map),
            pl.BlockSpec((None, tm, P_LOC_W), row_map),
            mod_spec(2), mod_spec(3), mod_spec(4), mod_spec(5),
            lay(1, D_MODEL),
            lay(8, CV_W),
            lay(1, GM_W),
            lay(GM_GROUPS, CHUNK, CHUNK),
            lay(CHUNK, GM_W),
            lay(D_MODEL, D_MODEL),
            lay(D_MODEL, D_FF),
            lay(D_MODEL, D_FF),
            lay(D_FF, D_MODEL),
            pl.BlockSpec((1, D_MODEL), lambda b, i: (0, 0)),
        ],
        out_specs=pl.BlockSpec((None, tm, D_MODEL), row_map),
        out_shape=jax.ShapeDtypeStruct((bsz, n_tiles * tm, D_MODEL), _F32),
        compiler_params=pltpu.CompilerParams(vmem_limit_bytes=_VMEM_LIMIT),
        name="out_ffn",
    )(xs, ml, p_loc, mods, mods, mods, mods, norm2, conv_w, gm_norm, gm_w, gm_b, w_out, w1, w3, w2, norm_f)


def _prepare_in_weights(w_in, b_gates):
    depth = w_in.shape[0]
    sl = lambda off, width: w_in[:, :, off:off + width]
    w_main = jnp.concatenate(
        [sl(_OFF_K, ML_W), sl(_OFF_V, ML_W), sl(_OFF_Q, ML_W), sl(_OFF_O, ML_W),
         sl(_OFF_CV, 3 * CV_W), sl(_OFF_GM, 2 * GM_W)], axis=-1).astype(_BF16)
    order = jnp.array([0, 2, 1, 3])[:, None] * ML_HEADS + jnp.arange(ML_HEADS)[None, :]
    order = order.reshape(-1)
    wg = jnp.take(sl(_OFF_G, N_GATES), order, axis=-1)
    w_gate = jnp.pad(wg, ((0, 0), (0, 0), (0, LANES - N_GATES))).astype(_BF16)
    gate_bias = jnp.pad(jnp.take(b_gates, order, axis=-1), ((0, 0), (0, LANES - N_GATES)))
    return w_main, w_gate, gate_bias.reshape(depth, 1, LANES).astype(_F32)


def kernel(x, c, ctx, c_ctx, w_ada, b_ada, norm1, norm2, w_in, b_gates, ml_norm, conv_w, gm_norm, gm_ws,
           gm_bs, w_out, w1, w3, w2, norm_f):
    bsz, seq, _ = x.shape
    ctx_len = ctx.shape[1]
    depth = w_in.shape[0]
    assert ctx_len == TILE_ROWS and seq % TILE_ROWS == 0 and bsz < MOD_ROWS
    assert w_in.shape[-1] == _D_IN and w1.shape[-1] == D_FF
    n_lat_tiles = seq // TILE_ROWS
    n_ctx_chunks = ctx_len // CHUNK

    cc = jnp.concatenate([c, c_ctx[None, :], jnp.zeros((MOD_ROWS - bsz - 1, D_MODEL), _F32)], axis=0)
    mods = _ada_table(cc, w_ada, b_ada)

    w_main, w_gate, gate_bias = _prepare_in_weights(w_in, b_gates)
    wo_b, w1_b, w3_b, w2_b = (w.astype(_BF16) for w in (w_out, w1, w3, w2))
    gm_w_b = gm_ws.astype(_BF16)
    gm_bias = jnp.repeat(jnp.swapaxes(gm_bs, 1, 2), GM_W // GM_GROUPS, axis=2)
    conv_w8 = jnp.pad(conv_w, ((0, 0), (0, 8 - conv_w.shape[1]), (0, 0)))
    norm1_r = norm1.reshape(depth, 1, D_MODEL)
    norm2_r = norm2.reshape(depth, 1, D_MODEL)
    gm_norm_r = gm_norm.reshape(depth, 1, GM_W)
    ml_norm_r = ml_norm.reshape(depth, 1, ML_W)
    norm_f_r = norm_f.reshape(1, D_MODEL)

    xs = jnp.concatenate([x, ctx], axis=1)
    for l in range(depth):
        last = l == depth - 1
        p_ml, p_loc, gates = _in_proj(xs, mods, l, norm1_r, w_main, w_gate,
                                      tile_lo=0, n_tiles=n_lat_tiles + 1, ctx_tile=n_lat_tiles)
        ml = _mlstm(p_ml, gates, gate_bias[l], ml_norm_r[l], n_ctx_chunks=n_ctx_chunks, ctx_out=not last)
        xs = _out_ffn(xs, ml, p_loc, mods, l, norm2_r, conv_w8, gm_norm_r, gm_w_b, gm_bias,
                      wo_b, w1_b, w3_b, w2_b, norm_f_r,
                      n_tiles=n_lat_tiles if last else n_lat_tiles + 1, ctx_tile=n_lat_tiles,
                      final_norm=last)
    return xs
```

```python
import functools

import jax
import jax.numpy as jnp
from jax import lax
from jax.experimental import pallas as pl
from jax.experimental.pallas import tpu as pltpu

D_MODEL = 1024
GRID_W = 64
ML_W = D_MODEL // 2
ML_HEADS = 4
ML_DH = ML_W // ML_HEADS
CHUNK = 128
CV_W = D_MODEL // 4
GM_W = D_MODEL // 4
GM_GROUPS = 4
N_GATES = 4 * ML_HEADS
D_FF = 2816
EPS = 1e-6
LANES = 128
SUBLANES = 8
N_DIRHEAD = 2 * ML_HEADS

_OFF_K = 0
_OFF_V = _OFF_K + ML_W
_OFF_G = _OFF_V + ML_W
_OFF_Q = _OFF_G + N_GATES
_OFF_O = _OFF_Q + ML_W
_OFF_CV = _OFF_O + ML_W
_OFF_GM = _OFF_CV + 3 * CV_W
_D_IN = _OFF_GM + 2 * GM_W

P_ML_W = 3 * ML_W
P_LOC_W = 3 * CV_W + 2 * GM_W
KG_ROWS = ML_W + N_GATES
MOD_ROWS = 24
TILE_ROWS = 768

_VMEM_LIMIT = 56 * 1024 * 1024

_BF16 = jnp.bfloat16
_F32 = jnp.float32


def _dot(a, b):
    return jnp.dot(a, b, preferred_element_type=_F32)


def _rms_scale(x):
    return lax.rsqrt(jnp.mean(x * x, axis=-1, keepdims=True) + EPS)


def _ada_kernel(c_ref, w_ref, b_ref, o_ref):
    c = c_ref[...]
    s = (c * jax.nn.sigmoid(c)).astype(_BF16)
    o_ref[...] = _dot(s, w_ref[...].astype(_BF16)) + b_ref[...]


def _ada_table(cc, w_ada, b_ada):
    depth, _, n = w_ada.shape
    tn = 1536
    return pl.pallas_call(
        _ada_kernel,
        grid=(depth, n // tn),
        in_specs=[
            pl.BlockSpec((MOD_ROWS, D_MODEL), lambda l, j: (0, 0)),
            pl.BlockSpec((None, D_MODEL, tn), lambda l, j: (l, 0, j)),
            pl.BlockSpec((None, 1, tn), lambda l, j: (l, 0, j)),
        ],
        out_specs=pl.BlockSpec((None, MOD_ROWS, tn), lambda l, j: (l, 0, j)),
        out_shape=jax.ShapeDtypeStruct((depth, MOD_ROWS, n), _F32),
        compiler_params=pltpu.CompilerParams(vmem_limit_bytes=_VMEM_LIMIT),
        name="ada_table",
    )(cc, w_ada, b_ada.reshape(depth, 1, n))


def _global_rows(tm):
    return pl.program_id(1) * tm + lax.broadcasted_iota(jnp.int32, (tm, 1), 0)


def _mod_rows(mod_ref, is_ctx_row, batch_rows):
    b = pl.program_id(0)
    return jnp.where(is_ctx_row, mod_ref[batch_rows:batch_rows + 1, :], mod_ref[pl.ds(b, 1), :])


def _in_proj_kernel(x_ref, sh_ref, sc_ref, nw_ref, w_ref, wkg_ref, pml_ref, ploc_ref, kt_ref, gi_ref, gf_ref,
                    *, seq, batch_rows):
    x = x_ref[...]
    is_ctx_row = _global_rows(x.shape[0]) >= seq
    xn = x * _rms_scale(x) * nw_ref[...]
    hx = xn * (1.0 + _mod_rows(sc_ref, is_ctx_row, batch_rows)) + _mod_rows(sh_ref, is_ctx_row, batch_rows)
    hb = hx.astype(_BF16)
    p = _dot(hb, w_ref[...])
    pml_ref[...] = p[:, :P_ML_W].astype(_BF16)
    ploc_ref[...] = p[:, P_ML_W:].astype(_BF16)
    kg = lax.dot_general(wkg_ref[...], hb, (((1,), (1,)), ((), ())), preferred_element_type=_F32)
    k_scale = ML_DH ** -0.5
    for n in range(x.shape[0] // CHUNK):
        lanes = slice(n * CHUNK, (n + 1) * CHUNK)
        kt_ref[n] = (kg[:ML_W, lanes] * k_scale).astype(_BF16)
        gi_ref[n * N_DIRHEAD:(n + 1) * N_DIRHEAD, :] = kg[ML_W:ML_W + N_DIRHEAD, lanes]
        gf_ref[n * N_DIRHEAD:(n + 1) * N_DIRHEAD, :] = kg[ML_W + N_DIRHEAD:, lanes]


def _in_proj(xs, mods, layer, norm_w, w_main, w_kg, *, seq):
    bsz, t, _ = xs.shape
    tm = TILE_ROWS
    cpt = tm // CHUNK
    n_chunks = t // CHUNK
    kern = functools.partial(_in_proj_kernel, seq=seq, batch_rows=bsz)
    row_map = lambda b, i: (b, i, 0)
    return pl.pallas_call(
        kern,
        grid=(bsz, t // tm),
        in_specs=[
            pl.BlockSpec((None, tm, D_MODEL), row_map),
            pl.BlockSpec((None, MOD_ROWS, D_MODEL), lambda b, i: (layer, 0, 0)),
            pl.BlockSpec((None, MOD_ROWS, D_MODEL), lambda b, i: (layer, 0, 1)),
            pl.BlockSpec((None, 1, D_MODEL), lambda b, i: (layer, 0, 0)),
            pl.BlockSpec((None, D_MODEL, P_ML_W + P_LOC_W), lambda b, i: (layer, 0, 0)),
            pl.BlockSpec((None, KG_ROWS, D_MODEL), lambda b, i: (layer, 0, 0)),
        ],
        out_specs=[
            pl.BlockSpec((None, tm, P_ML_W), row_map),
            pl.BlockSpec((None, tm, P_LOC_W), row_map),
            pl.BlockSpec((None, cpt, ML_W, CHUNK), lambda b, i: (b, i, 0, 0)),
            pl.BlockSpec((None, cpt * N_DIRHEAD, CHUNK), row_map),
            pl.BlockSpec((None, cpt * N_DIRHEAD, CHUNK), row_map),
        ],
        out_shape=[
            jax.ShapeDtypeStruct((bsz, t, P_ML_W), _BF16),
            jax.ShapeDtypeStruct((bsz, t, P_LOC_W), _BF16),
            jax.ShapeDtypeStruct((bsz, n_chunks, ML_W, CHUNK), _BF16),
            jax.ShapeDtypeStruct((bsz, n_chunks * N_DIRHEAD, CHUNK), _F32),
            jax.ShapeDtypeStruct((bsz, n_chunks * N_DIRHEAD, CHUNK), _F32),
        ],
        compiler_params=pltpu.CompilerParams(vmem_limit_bytes=_VMEM_LIMIT),
        name="in_proj",
    )(xs, mods, mods, norm_w, w_main, w_kg)


def _directional_scan(x, combine, fill):
    row = lax.broadcasted_iota(jnp.int32, x.shape, 0)
    lane = lax.broadcasted_iota(jnp.int32, x.shape, 1)
    fwd = (row & (N_DIRHEAD - 1)) < ML_HEADS
    n = x.shape[1]
    s = 1
    while s < n:
        prev = jnp.where(lane >= s, pltpu.roll(x, s, 1), fill)
        nxt = jnp.where(lane < n - s, pltpu.roll(x, n - s, 1), fill)
        x = combine(x, jnp.where(fwd, prev, nxt))
        s *= 2
    return x


def _mlstm_kernel(p_ref, kt_ref, gi_ref, gf_ref, bi_ref, bf_ref, mln_ref, o_ref,
                  r_s, cm_s, b_s, bsum_s, rmax_s, cn_s, rhs_s, kti_s, hf_s, hb_s,
                  *, n_chunks, n_ctx_chunks, ctx_out):
    L = CHUNK
    R = N_DIRHEAD
    n_lat = n_chunks - n_ctx_chunks
    neg_inf = -jnp.inf

    tile_rows = lambda a: jnp.concatenate([a] * n_chunks, axis=0)
    li = gi_ref[...] + tile_rows(bi_ref[...])
    lf = jax.nn.log_sigmoid(gf_ref[...] + tile_rows(bf_ref[...]))
    b = _directional_scan(lf, jnp.add, 0.0)
    r = li - b
    r_s[...] = r
    b_s[...] = b
    cm_s[...] = _directional_scan(r, jnp.maximum, neg_inf)
    bsum_s[...] = jnp.broadcast_to(jnp.sum(lf, axis=1, keepdims=True), lf.shape)
    rmax_s[...] = jnp.broadcast_to(jnp.max(r, axis=1, keepdims=True), lf.shape)

    cn_s[...] = jnp.zeros_like(cn_s)
    eye = (lax.broadcasted_iota(jnp.int32, (ML_DH, ML_DH), 0)
           == lax.broadcasted_iota(jnp.int32, (ML_DH, ML_DH), 1)).astype(_BF16)
    for j in range(R):
        rhs_s[j, 0:L, ML_DH:] = jnp.ones((L, ML_DH), _BF16)
        rhs_s[j, L:, :] = jnp.zeros((ML_DH, 2 * ML_DH), _BF16)
        kti_s[j, :, L:] = eye

    def make_step(with_out):
        def step(i, m_prev):
            cf = jnp.where(i < n_ctx_chunks, n_lat + i, i - n_ctx_chunks)
            cb = n_chunks - 1 - i
            fwd_rows = lax.broadcasted_iota(jnp.int32, (R, L), 0) < ML_HEADS
            rows_f = pl.ds(pl.multiple_of(cf * R, R), R)
            rows_b = pl.ds(pl.multiple_of(cb * R, R), R)
            pick = lambda ref: jnp.where(fwd_rows, ref[rows_f, :], ref[rows_b, :])
            r8 = pick(r_s)
            rmax = pick(rmax_s)
            a8 = jnp.maximum(m_prev, pick(cm_s))
            e_negm = jnp.exp(-(pick(b_s) + a8))
            a_last = jnp.maximum(m_prev, rmax)
            decay = jnp.exp(m_prev - a_last)
            f_kw = jnp.exp(r8 - rmax) * jnp.exp(rmax - a_last)
            m_new = pick(bsum_s) + a_last
            pad_rows = jnp.zeros((LANES - 2 * R, L), _F32)
            cols = jnp.concatenate([a8, e_negm, pad_rows], axis=0).T
            if with_out:
                t_idx = lax.broadcasted_iota(jnp.int32, (L, L), 0)
                s_idx = lax.broadcasted_iota(jnp.int32, (L, L), 1)
                causal = (s_idx <= t_idx, s_idx >= t_idx)

            chunk_of = lambda j: cf if j < ML_HEADS else cb
            rows_of = lambda j: pl.ds(pl.multiple_of(chunk_of(j) * L, L), L)
            head_of = lambda j: slice((j % ML_HEADS) * ML_DH, (j % ML_HEADS + 1) * ML_DH)
            kts, qk_qs, s_exts, zs, upds = [], [], [], [], []
            for j in range(R):
                kt = kt_ref[chunk_of(j), head_of(j), :]
                kts.append(kt)
                rhs_s[j, 0:L, 0:ML_DH] = p_ref[rows_of(j), head_of(j)]
                if with_out:
                    kti_s[j, :, 0:L] = kt
                    h = j % ML_HEADS
                    q = p_ref[rows_of(j), ML_W + h * ML_DH:ML_W + (h + 1) * ML_DH]
                    qk_qs.append(_dot(q, kti_s[j]))
            if with_out:
                for j in range(R):
                    a_col = jnp.broadcast_to(cols[:, j:j + 1], (L, L))
                    logw = jnp.concatenate(
                        [jnp.where(causal[j // ML_HEADS], r8[j:j + 1, :] - a_col, neg_inf),
                         m_prev[j:j + 1, :] - a_col], axis=1)
                    s_exts.append((qk_qs[j] * jnp.exp(logw)).astype(_BF16))
                for j in range(R):
                    zs.append(_dot(s_exts[j], rhs_s[j]))
                for j in range(R):
                    z = zs[j]
                    hd_s = hf_s if j < ML_HEADS else hb_s
                    floor = jnp.broadcast_to(cols[:, R + j:R + j + 1], (L, ML_DH))
                    hd_s[rows_of(j), head_of(j)] = z[:, :ML_DH] / jnp.maximum(jnp.abs(z[:, ML_DH:]), floor)
            for j in range(R):
                kw = (kts[j].astype(_F32) * f_kw[j:j + 1, :]).astype(_BF16)
                upds.append(_dot(kw, rhs_s[j, 0:L, :]))
            for j in range(R):
                cn = decay[j:j + 1, 0:1] * cn_s[j] + upds[j]
                cn_s[j] = cn
                rhs_s[j, L:, :] = cn.astype(_BF16)
            return m_new
        return step

    m0 = jnp.zeros((R, L), _F32)
    m1 = lax.fori_loop(0, n_ctx_chunks, make_step(ctx_out), m0)
    lax.fori_loop(n_ctx_chunks, n_chunks, make_step(True), m1)

    def finish(c, carry):
        rows = pl.ds(pl.multiple_of(c * L, L), L)
        for h in range(ML_HEADS):
            hs = slice(h * ML_DH, (h + 1) * ML_DH)
            hh = hf_s[rows, hs] + hb_s[rows, hs]
            hn = hh * _rms_scale(hh) * mln_ref[:, hs]
            o = jax.nn.sigmoid(p_ref[rows, 2 * ML_W + h * ML_DH:2 * ML_W + (h + 1) * ML_DH].astype(_F32))
            o_ref[rows, hs] = (o * hn).astype(_BF16)
        return carry

    lax.fori_loop(0, n_chunks if ctx_out else n_lat, finish, 0)
    if not ctx_out:
        o_ref[n_lat * L:, :] = jnp.zeros((n_ctx_chunks * L, ML_W), _BF16)


def _mlstm(p_ml, k_t, g_in, g_fg, bias_in, bias_fg, ml_norm, *, n_ctx_chunks, ctx_out):
    bsz, t, _ = p_ml.shape
    n_chunks = t // CHUNK
    kern = functools.partial(_mlstm_kernel, n_chunks=n_chunks, n_ctx_chunks=n_ctx_chunks, ctx_out=ctx_out)
    gate_rows = n_chunks * N_DIRHEAD
    gate_scratch = pltpu.VMEM((gate_rows, CHUNK), _F32)
    return pl.pallas_call(
        kern,
        grid=(bsz,),
        in_specs=[
            pl.BlockSpec((None, t, P_ML_W), lambda b: (b, 0, 0)),
            pl.BlockSpec((None, n_chunks, ML_W, CHUNK), lambda b: (b, 0, 0, 0)),
            pl.BlockSpec((None, gate_rows, CHUNK), lambda b: (b, 0, 0)),
            pl.BlockSpec((None, gate_rows, CHUNK), lambda b: (b, 0, 0)),
            pl.BlockSpec((N_DIRHEAD, CHUNK), lambda b: (0, 0)),
            pl.BlockSpec((N_DIRHEAD, CHUNK), lambda b: (0, 0)),
            pl.BlockSpec((1, ML_W), lambda b: (0, 0)),
        ],
        out_specs=pl.BlockSpec((None, t, ML_W), lambda b: (b, 0, 0)),
        out_shape=jax.ShapeDtypeStruct((bsz, t, ML_W), _BF16),
        scratch_shapes=[
            gate_scratch,
            gate_scratch,
            gate_scratch,
            gate_scratch,
            gate_scratch,
            pltpu.VMEM((N_DIRHEAD, ML_DH, 2 * ML_DH), _F32),
            pltpu.VMEM((N_DIRHEAD, CHUNK + ML_DH, 2 * ML_DH), _BF16),
            pltpu.VMEM((N_DIRHEAD, ML_DH, CHUNK + ML_DH), _BF16),
            pltpu.VMEM((t, ML_W), _F32),
            pltpu.VMEM((t, ML_W), _F32),
        ],
        compiler_params=pltpu.CompilerParams(vmem_limit_bytes=_VMEM_LIMIT),
        name="mlstm",
    )(p_ml, k_t, g_in, g_fg, bias_in, bias_fg, ml_norm)


def _out_ffn_kernel(x_ref, ml_ref, ploc_ref, g1_ref, sh2_ref, sc2_ref, g2_ref, n2_ref, cw_ref, gmn_ref,
                    gmw_ref, gmb_ref, wo_ref, w1_ref, w3_ref, w2_ref, nf_ref, o_ref,
                    *, seq, total, batch_rows, final_norm):
    tm = x_ref.shape[0]
    g_row = _global_rows(tm)
    is_ctx_row = g_row >= seq
    mod = lambda ref: _mod_rows(ref, is_ctx_row, batch_rows)

    gate_b = ploc_ref[:, 0:CV_W].astype(_F32)
    z = ploc_ref[:, CV_W:2 * CV_W].astype(_F32) * ploc_ref[:, 2 * CV_W:3 * CV_W].astype(_F32)
    pos = g_row & (GRID_W - 1)
    first = (pos == 0) & (g_row <= seq)
    last = ((pos == GRID_W - 1) & (g_row < seq)) | (g_row == total - 1)
    z_prev = jnp.where(first, 0.0, pltpu.roll(z, 1, 0))
    z_next = jnp.where(last, 0.0, pltpu.roll(z, tm - 1, 0))
    conv = gate_b * (cw_ref[0:1, :] * z_prev + cw_ref[1:2, :] * z + cw_ref[2:3, :] * z_next)

    u = ploc_ref[:, 3 * CV_W:3 * CV_W + GM_W].astype(_F32)
    v = ploc_ref[:, 3 * CV_W + GM_W:3 * CV_W + 2 * GM_W].astype(_F32)
    vn = (v * _rms_scale(v) * gmn_ref[...]).astype(_BF16)
    group = lax.broadcasted_iota(jnp.int32, (CHUNK, GM_W), 1) // (GM_W // GM_GROUPS)
    zs = []
    for n in range(tm // CHUNK):
        vc = vn[n * CHUNK:(n + 1) * CHUNK]
        zc = gmb_ref[...]
        for g in range(GM_GROUPS):
            zc = zc + jnp.where(group == g, _dot(gmw_ref[g], vc), 0.0)
        zs.append(zc)
    gm = u * jnp.concatenate(zs, axis=0)

    mix = jnp.concatenate([ml_ref[...], conv.astype(_BF16), gm.astype(_BF16)], axis=1)
    x1 = x_ref[...] + mod(g1_ref) * _dot(mix, wo_ref[...])

    hx2 = x1 * _rms_scale(x1) * n2_ref[...]
    hx2 = (hx2 * (1.0 + mod(sc2_ref)) + mod(sh2_ref)).astype(_BF16)
    a = _dot(hx2, w1_ref[...])
    act = (a * jax.nn.sigmoid(a) * _dot(hx2, w3_ref[...])).astype(_BF16)
    x2 = x1 + mod(g2_ref) * _dot(act, w2_ref[...])
    if final_norm:
        x2 = x2 * _rms_scale(x2) * nf_ref[...]
    o_ref[...] = x2


def _out_ffn(xs, ml, p_loc, mods, layer, norm2, conv_w, gm_norm, gm_w, gm_b, w_out, w1, w3, w2, norm_f,
             *, seq, out_rows, final_norm):
    bsz, t, _ = xs.shape
    tm = TILE_ROWS
    kern = functools.partial(_out_ffn_kernel, seq=seq, total=t, batch_rows=bsz, final_norm=final_norm)
    row_map = lambda b, i: (b, i, 0)
    mod_spec = lambda k: pl.BlockSpec((None, MOD_ROWS, D_MODEL), lambda b, i: (layer, 0, k))
    lay = lambda *shape: pl.BlockSpec((None,) + shape, lambda b, i: (layer,) + (0,) * len(shape))
    return pl.pallas_call(
        kern,
        grid=(bsz, pl.cdiv(out_rows, tm)),
        in_specs=[
            pl.BlockSpec((None, tm, D_MODEL), row_map),
            pl.BlockSpec((None, tm, ML_W), row_map),
            pl.BlockSpec((None, tm, P_LOC_W), row_map),
            mod_spec(2), mod_spec(3), mod_spec(4), mod_spec(5),
            lay(1, D_MODEL),
            lay(SUBLANES, CV_W),
            lay(1, GM_W),
            lay(GM_GROUPS, CHUNK, CHUNK),
            lay(CHUNK, GM_W),
            lay(D_MODEL, D_MODEL),
            lay(D_MODEL, D_FF),
            lay(D_MODEL, D_FF),
            lay(D_FF, D_MODEL),
            pl.BlockSpec((1, D_MODEL), lambda b, i: (0, 0)),
        ],
        out_specs=pl.BlockSpec((None, tm, D_MODEL), row_map),
        out_shape=jax.ShapeDtypeStruct((bsz, out_rows, D_MODEL), _F32),
        compiler_params=pltpu.CompilerParams(vmem_limit_bytes=_VMEM_LIMIT),
        name="out_ffn",
    )(xs, ml, p_loc, mods, mods, mods, mods, norm2, conv_w, gm_norm, gm_w, gm_b, w_out, w1, w3, w2, norm_f)


def _prepare_in_weights(w_in, b_gates):
    sl = lambda off, width: w_in[:, :, off:off + width]
    w_main = jnp.concatenate(
        [sl(_OFF_V, ML_W), sl(_OFF_Q, ML_W), sl(_OFF_O, ML_W), sl(_OFF_CV, 3 * CV_W), sl(_OFF_GM, 2 * GM_W)],
        axis=-1).astype(_BF16)
    order = (jnp.array([0, 2, 1, 3])[:, None] * ML_HEADS + jnp.arange(ML_HEADS)[None, :]).reshape(-1)
    w_kg = jnp.concatenate([sl(_OFF_K, ML_W), jnp.take(sl(_OFF_G, N_GATES), order, axis=-1)], axis=-1)
    w_kg = jnp.swapaxes(w_kg, 1, 2).astype(_BF16)
    bias = jnp.take(b_gates, order, axis=-1).astype(_F32)
    bias = jnp.broadcast_to(bias[:, :, None], bias.shape + (CHUNK,))
    return w_main, w_kg, bias[:, :N_DIRHEAD], bias[:, N_DIRHEAD:]


def kernel(x, c, ctx, c_ctx, w_ada, b_ada, norm1, norm2, w_in, b_gates, ml_norm, conv_w, gm_norm, gm_ws,
           gm_bs, w_out, w1, w3, w2, norm_f):
    bsz, seq, _ = x.shape
    ctx_len = ctx.shape[1]
    depth = w_in.shape[0]
    total = seq + ctx_len
    assert bsz < MOD_ROWS and w_in.shape[-1] == _D_IN and w1.shape[-1] == D_FF
    assert seq % GRID_W == 0 and seq % CHUNK == 0 and ctx_len % CHUNK == 0
    assert total % TILE_ROWS == 0 and TILE_ROWS % CHUNK == 0 and total - TILE_ROWS <= seq
    n_ctx_chunks = ctx_len // CHUNK

    cc = jnp.concatenate([c, c_ctx[None, :], jnp.zeros((MOD_ROWS - bsz - 1, D_MODEL), _F32)], axis=0)
    mods = _ada_table(cc, w_ada, b_ada)

    w_main, w_kg, bias_in, bias_fg = _prepare_in_weights(w_in, b_gates)
    wo_b, w1_b, w3_b, w2_b = (w.astype(_BF16) for w in (w_out, w1, w3, w2))
    gm_w_b = gm_ws.astype(_BF16)
    gm_bias = jnp.repeat(jnp.swapaxes(gm_bs, 1, 2), GM_W // GM_GROUPS, axis=2)
    conv_w8 = jnp.pad(conv_w, ((0, 0), (0, SUBLANES - conv_w.shape[1]), (0, 0)))
    norm1_r = norm1.reshape(depth, 1, D_MODEL)
    norm2_r = norm2.reshape(depth, 1, D_MODEL)
    gm_norm_r = gm_norm.reshape(depth, 1, GM_W)
    ml_norm_r = ml_norm.reshape(depth, 1, ML_W)
    norm_f_r = norm_f.reshape(1, D_MODEL)

    xs = jnp.concatenate([x, ctx], axis=1)
    for l in range(depth):
        last = l == depth - 1
        p_ml, p_loc, k_t, g_in, g_fg = _in_proj(xs, mods, l, norm1_r, w_main, w_kg, seq=seq)
        ml = _mlstm(p_ml, k_t, g_in, g_fg, bias_in[l], bias_fg[l], ml_norm_r[l],
                    n_ctx_chunks=n_ctx_chunks, ctx_out=not last)
        xs = _out_ffn(xs, ml, p_loc, mods, l, norm2_r, conv_w8, gm_norm_r, gm_w_b, gm_bias,
                      wo_b, w1_b, w3_b, w2_b, norm_f_r,
                      seq=seq, out_rows=seq if last else total, final_norm=last)
    return xs
```

```python
import functools

import jax
import jax.numpy as jnp
from jax import lax
from jax.experimental import pallas as pl
from jax.experimental.pallas import tpu as pltpu

D_MODEL = 1024
GRID_W = 64
ML_W = D_MODEL // 2
ML_HEADS = 4
ML_DH = ML_W // ML_HEADS
CHUNK = 128
CV_W = D_MODEL // 4
GM_W = D_MODEL // 4
GM_GROUPS = 4
N_GATES = 4 * ML_HEADS
D_FF = 2816
EPS = 1e-6
LANES = 128
SUBLANES = 8
N_DIRHEAD = 2 * ML_HEADS

_OFF_K = 0
_OFF_V = _OFF_K + ML_W
_OFF_G = _OFF_V + ML_W
_OFF_Q = _OFF_G + N_GATES
_OFF_O = _OFF_Q + ML_W
_OFF_CV = _OFF_O + ML_W
_OFF_GM = _OFF_CV + 3 * CV_W
_D_IN = _OFF_GM + 2 * GM_W

P_ML_W = 3 * ML_W
P_LOC_W = 3 * CV_W + 2 * GM_W
KG_ROWS = ML_W + N_GATES
MOD_ROWS = 24
TILE_ROWS = 768
LAST_TILE_ROWS = 1024
SUB_ROWS = 256

_VMEM_LIMIT = 56 * 1024 * 1024

_BF16 = jnp.bfloat16
_F32 = jnp.float32


def _dot(a, b):
    return jnp.dot(a, b, preferred_element_type=_F32)


def _rms_scale(x):
    return lax.rsqrt(jnp.mean(x * x, axis=-1, keepdims=True) + EPS)


def _ada_kernel(c_ref, w_ref, b_ref, o_ref):
    c = c_ref[...]
    s = (c * jax.nn.sigmoid(c)).astype(_BF16)
    o_ref[...] = _dot(s, w_ref[...].astype(_BF16)) + b_ref[...]


def _ada_table(cc, w_ada, b_ada):
    depth, _, n = w_ada.shape
    tn = 1536
    return pl.pallas_call(
        _ada_kernel,
        grid=(depth, n // tn),
        in_specs=[
            pl.BlockSpec((MOD_ROWS, D_MODEL), lambda l, j: (0, 0)),
            pl.BlockSpec((None, D_MODEL, tn), lambda l, j: (l, 0, j)),
            pl.BlockSpec((None, 1, tn), lambda l, j: (l, 0, j)),
        ],
        out_specs=pl.BlockSpec((None, MOD_ROWS, tn), lambda l, j: (l, 0, j)),
        out_shape=jax.ShapeDtypeStruct((depth, MOD_ROWS, n), _F32),
        compiler_params=pltpu.CompilerParams(vmem_limit_bytes=_VMEM_LIMIT),
        name="ada_table",
    )(cc, w_ada, b_ada.reshape(depth, 1, n))


def _skewed_order(n_sub, n_stages):
    return [(k, t - k) for t in range(n_sub + n_stages - 1) for k in range(n_stages) if 0 <= t - k < n_sub]


def _global_rows(tm):
    return pl.program_id(1) * tm + lax.broadcasted_iota(jnp.int32, (tm, 1), 0)


def _mod_rows(mod_ref, is_ctx_row, batch_rows):
    b = pl.program_id(0)
    return jnp.where(is_ctx_row, mod_ref[batch_rows:batch_rows + 1, :], mod_ref[pl.ds(b, 1), :])


def _in_proj_kernel(x_ref, sh_ref, sc_ref, nw_ref, w_ref, wkg_ref, pml_ref, ploc_ref, kt_ref, gi_ref, gf_ref,
                    *, seq, batch_rows):
    tm = x_ref.shape[0]
    sb = SUB_ROWS
    cps = sb // CHUNK
    k_scale = ML_DH ** -0.5

    def norm_stage(s):
        rs = slice(s * sb, (s + 1) * sb)
        x = x_ref[rs, :]
        is_ctx_row = _global_rows(tm)[rs] >= seq
        xn = x * _rms_scale(x) * nw_ref[...]
        hx = xn * (1.0 + _mod_rows(sc_ref, is_ctx_row, batch_rows)) + _mod_rows(sh_ref, is_ctx_row, batch_rows)
        return hx.astype(_BF16)

    def proj_stage(s, hb):
        rs = slice(s * sb, (s + 1) * sb)
        p = _dot(hb, w_ref[...])
        pml_ref[rs, :] = p[:, :P_ML_W].astype(_BF16)
        ploc_ref[rs, :] = p[:, P_ML_W:].astype(_BF16)
        kg = lax.dot_general(wkg_ref[...], hb, (((1,), (1,)), ((), ())), preferred_element_type=_F32)
        for n in range(cps):
            c = s * cps + n
            lanes = slice(n * CHUNK, (n + 1) * CHUNK)
            kt_ref[c] = (kg[:ML_W, lanes] * k_scale).astype(_BF16)
            gi_ref[c * N_DIRHEAD:(c + 1) * N_DIRHEAD, :] = kg[ML_W:ML_W + N_DIRHEAD, lanes]
            gf_ref[c * N_DIRHEAD:(c + 1) * N_DIRHEAD, :] = kg[ML_W + N_DIRHEAD:, lanes]

    hb = {}
    for k, s in _skewed_order(tm // sb, 2):
        if k == 0:
            hb[s] = norm_stage(s)
        else:
            proj_stage(s, hb.pop(s))


def _in_proj(xs, mods, layer, norm_w, w_main, w_kg, *, seq):
    bsz, t, _ = xs.shape
    tm = TILE_ROWS
    cpt = tm // CHUNK
    n_chunks = t // CHUNK
    kern = functools.partial(_in_proj_kernel, seq=seq, batch_rows=bsz)
    row_map = lambda b, i: (b, i, 0)
    return pl.pallas_call(
        kern,
        grid=(bsz, t // tm),
        in_specs=[
            pl.BlockSpec((None, tm, D_MODEL), row_map),
            pl.BlockSpec((None, MOD_ROWS, D_MODEL), lambda b, i: (layer, 0, 0)),
            pl.BlockSpec((None, MOD_ROWS, D_MODEL), lambda b, i: (layer, 0, 1)),
            pl.BlockSpec((None, 1, D_MODEL), lambda b, i: (layer, 0, 0)),
            pl.BlockSpec((None, D_MODEL, P_ML_W + P_LOC_W), lambda b, i: (layer, 0, 0)),
            pl.BlockSpec((None, KG_ROWS, D_MODEL), lambda b, i: (layer, 0, 0)),
        ],
        out_specs=[
            pl.BlockSpec((None, tm, P_ML_W), row_map),
            pl.BlockSpec((None, tm, P_LOC_W), row_map),
            pl.BlockSpec((None, cpt, ML_W, CHUNK), lambda b, i: (b, i, 0, 0)),
            pl.BlockSpec((None, cpt * N_DIRHEAD, CHUNK), row_map),
            pl.BlockSpec((None, cpt * N_DIRHEAD, CHUNK), row_map),
        ],
        out_shape=[
            jax.ShapeDtypeStruct((bsz, t, P_ML_W), _BF16),
            jax.ShapeDtypeStruct((bsz, t, P_LOC_W), _BF16),
            jax.ShapeDtypeStruct((bsz, n_chunks, ML_W, CHUNK), _BF16),
            jax.ShapeDtypeStruct((bsz, n_chunks * N_DIRHEAD, CHUNK), _F32),
            jax.ShapeDtypeStruct((bsz, n_chunks * N_DIRHEAD, CHUNK), _F32),
        ],
        compiler_params=pltpu.CompilerParams(vmem_limit_bytes=_VMEM_LIMIT),
        name="in_proj",
    )(xs, mods, mods, norm_w, w_main, w_kg)


def _directional_scan(x, combine, fill):
    row = lax.broadcasted_iota(jnp.int32, x.shape, 0)
    lane = lax.broadcasted_iota(jnp.int32, x.shape, 1)
    fwd = (row & (N_DIRHEAD - 1)) < ML_HEADS
    n = x.shape[1]
    s = 1
    while s < n:
        prev = jnp.where(lane >= s, pltpu.roll(x, s, 1), fill)
        nxt = jnp.where(lane < n - s, pltpu.roll(x, n - s, 1), fill)
        x = combine(x, jnp.where(fwd, prev, nxt))
        s *= 2
    return x


def _mlstm_kernel(p_ref, kt_ref, gi_ref, gf_ref, bi_ref, bf_ref, mln_ref, o_ref,
                  r_s, cm_s, b_s, bsum_s, rmax_s, cn_s, rhs_s, kti_s, hf_s, hb_s,
                  *, n_chunks, n_ctx_chunks, ctx_out):
    L = CHUNK
    R = N_DIRHEAD
    n_lat = n_chunks - n_ctx_chunks
    neg_inf = -jnp.inf

    tile_rows = lambda a: jnp.concatenate([a] * n_chunks, axis=0)
    li = gi_ref[...] + tile_rows(bi_ref[...])
    lf = jax.nn.log_sigmoid(gf_ref[...] + tile_rows(bf_ref[...]))
    b = _directional_scan(lf, jnp.add, 0.0)
    r = li - b
    r_s[...] = r
    b_s[...] = b
    cm_s[...] = _directional_scan(r, jnp.maximum, neg_inf)
    bsum_s[...] = jnp.broadcast_to(jnp.sum(lf, axis=1, keepdims=True), lf.shape)
    rmax_s[...] = jnp.broadcast_to(jnp.max(r, axis=1, keepdims=True), lf.shape)

    cn_s[...] = jnp.zeros_like(cn_s)
    eye = (lax.broadcasted_iota(jnp.int32, (ML_DH, ML_DH), 0)
           == lax.broadcasted_iota(jnp.int32, (ML_DH, ML_DH), 1)).astype(_BF16)
    for j in range(R):
        rhs_s[j, 0:L, ML_DH:] = jnp.ones((L, ML_DH), _BF16)
        rhs_s[j, L:, :] = jnp.zeros((ML_DH, 2 * ML_DH), _BF16)
        kti_s[j, :, L:] = eye

    def make_step(with_out):
        def step(i, m_prev):
            cf = jnp.where(i < n_ctx_chunks, n_lat + i, i - n_ctx_chunks)
            cb = n_chunks - 1 - i
            fwd_rows = lax.broadcasted_iota(jnp.int32, (R, L), 0) < ML_HEADS
            rows_f = pl.ds(pl.multiple_of(cf * R, R), R)
            rows_b = pl.ds(pl.multiple_of(cb * R, R), R)
            pick = lambda ref: jnp.where(fwd_rows, ref[rows_f, :], ref[rows_b, :])
            r8 = pick(r_s)
            rmax = pick(rmax_s)
            a8 = jnp.maximum(m_prev, pick(cm_s))
            e_negm = jnp.exp(-(pick(b_s) + a8))
            a_last = jnp.maximum(m_prev, rmax)
            decay = jnp.exp(m_prev - a_last)
            f_kw = jnp.exp(r8 - rmax) * jnp.exp(rmax - a_last)
            m_new = pick(bsum_s) + a_last
            pad_rows = jnp.zeros((LANES - 2 * R, L), _F32)
            cols = jnp.concatenate([a8, e_negm, pad_rows], axis=0).T
            if with_out:
                t_idx = lax.broadcasted_iota(jnp.int32, (L, L), 0)
                s_idx = lax.broadcasted_iota(jnp.int32, (L, L), 1)
                causal = (s_idx <= t_idx, s_idx >= t_idx)

            chunk_of = lambda j: cf if j < ML_HEADS else cb
            rows_of = lambda j: pl.ds(pl.multiple_of(chunk_of(j) * L, L), L)
            head_of = lambda j: slice((j % ML_HEADS) * ML_DH, (j % ML_HEADS + 1) * ML_DH)
            kts, qk_qs, s_exts, zs, upds = [], [], [], [], []
            for j in range(R):
                kt = kt_ref[chunk_of(j), head_of(j), :]
                kts.append(kt)
                rhs_s[j, 0:L, 0:ML_DH] = p_ref[rows_of(j), head_of(j)]
                if with_out:
                    kti_s[j, :, 0:L] = kt
                    h = j % ML_HEADS
                    q = p_ref[rows_of(j), ML_W + h * ML_DH:ML_W + (h + 1) * ML_DH]
                    qk_qs.append(_dot(q, kti_s[j]))
            if with_out:
                for j in range(R):
                    a_col = jnp.broadcast_to(cols[:, j:j + 1], (L, L))
                    logw = jnp.concatenate(
                        [jnp.where(causal[j // ML_HEADS], r8[j:j + 1, :] - a_col, neg_inf),
                         m_prev[j:j + 1, :] - a_col], axis=1)
                    s_exts.append((qk_qs[j] * jnp.exp(logw)).astype(_BF16))
                for j in range(R):
                    zs.append(_dot(s_exts[j], rhs_s[j]))
                for j in range(R):
                    z = zs[j]
                    hd_s = hf_s if j < ML_HEADS else hb_s
                    floor = jnp.broadcast_to(cols[:, R + j:R + j + 1], (L, ML_DH))
                    hd_s[rows_of(j), head_of(j)] = z[:, :ML_DH] / jnp.maximum(jnp.abs(z[:, ML_DH:]), floor)
            for j in range(R):
                kw = (kts[j].astype(_F32) * f_kw[j:j + 1, :]).astype(_BF16)
                upds.append(_dot(kw, rhs_s[j, 0:L, :]))
            for j in range(R):
                cn = decay[j:j + 1, 0:1] * cn_s[j] + upds[j]
                cn_s[j] = cn
                rhs_s[j, L:, :] = cn.astype(_BF16)
            return m_new
        return step

    m0 = jnp.zeros((R, L), _F32)
    m1 = lax.fori_loop(0, n_ctx_chunks, make_step(ctx_out), m0)
    lax.fori_loop(n_ctx_chunks, n_chunks, make_step(True), m1)

    def finish(c, carry):
        rows = pl.ds(pl.multiple_of(c * L, L), L)
        for h in range(ML_HEADS):
            hs = slice(h * ML_DH, (h + 1) * ML_DH)
            hh = hf_s[rows, hs] + hb_s[rows, hs]
            hn = hh * _rms_scale(hh) * mln_ref[:, hs]
            o = jax.nn.sigmoid(p_ref[rows, 2 * ML_W + h * ML_DH:2 * ML_W + (h + 1) * ML_DH].astype(_F32))
            o_ref[rows, hs] = (o * hn).astype(_BF16)
        return carry

    lax.fori_loop(0, n_chunks if ctx_out else n_lat, finish, 0)
    if not ctx_out:
        o_ref[n_lat * L:, :] = jnp.zeros((n_ctx_chunks * L, ML_W), _BF16)


def _mlstm(p_ml, k_t, g_in, g_fg, bias_in, bias_fg, ml_norm, *, n_ctx_chunks, ctx_out):
    bsz, t, _ = p_ml.shape
    n_chunks = t // CHUNK
    kern = functools.partial(_mlstm_kernel, n_chunks=n_chunks, n_ctx_chunks=n_ctx_chunks, ctx_out=ctx_out)
    gate_rows = n_chunks * N_DIRHEAD
    gate_scratch = pltpu.VMEM((gate_rows, CHUNK), _F32)
    return pl.pallas_call(
        kern,
        grid=(bsz,),
        in_specs=[
            pl.BlockSpec((None, t, P_ML_W), lambda b: (b, 0, 0)),
            pl.BlockSpec((None, n_chunks, ML_W, CHUNK), lambda b: (b, 0, 0, 0)),
            pl.BlockSpec((None, gate_rows, CHUNK), lambda b: (b, 0, 0)),
            pl.BlockSpec((None, gate_rows, CHUNK), lambda b: (b, 0, 0)),
            pl.BlockSpec((N_DIRHEAD, CHUNK), lambda b: (0, 0)),
            pl.BlockSpec((N_DIRHEAD, CHUNK), lambda b: (0, 0)),
            pl.BlockSpec((1, ML_W), lambda b: (0, 0)),
        ],
        out_specs=pl.BlockSpec((None, t, ML_W), lambda b: (b, 0, 0)),
        out_shape=jax.ShapeDtypeStruct((bsz, t, ML_W), _BF16),
        scratch_shapes=[
            gate_scratch,
            gate_scratch,
            gate_scratch,
            gate_scratch,
            gate_scratch,
            pltpu.VMEM((N_DIRHEAD, ML_DH, 2 * ML_DH), _F32),
            pltpu.VMEM((N_DIRHEAD, CHUNK + ML_DH, 2 * ML_DH), _BF16),
            pltpu.VMEM((N_DIRHEAD, ML_DH, CHUNK + ML_DH), _BF16),
            pltpu.VMEM((t, ML_W), _F32),
            pltpu.VMEM((t, ML_W), _F32),
        ],
        compiler_params=pltpu.CompilerParams(vmem_limit_bytes=_VMEM_LIMIT),
        name="mlstm",
    )(p_ml, k_t, g_in, g_fg, bias_in, bias_fg, ml_norm)


def _out_ffn_kernel(x_ref, ml_ref, ploc_ref, g1_ref, sh2_ref, sc2_ref, g2_ref, n2_ref, cw_ref, gmn_ref,
                    gmw_ref, gmb_ref, wo_ref, w1_ref, w3_ref, w2_ref, nf_ref, o_ref,
                    *, seq, total, batch_rows, final_norm):
    tm = x_ref.shape[0]
    sb = SUB_ROWS
    n_sub = tm // sb
    group = lax.broadcasted_iota(jnp.int32, (CHUNK, GM_W), 1) // (GM_W // GM_GROUPS)

    def rows_of(s):
        return slice(s * sb, (s + 1) * sb)

    def mods_of(s):
        g_row = _global_rows(tm)[rows_of(s)]
        is_ctx_row = g_row >= seq
        return g_row, (lambda ref: _mod_rows(ref, is_ctx_row, batch_rows))

    def mix_stage(s):
        rs = rows_of(s)
        g_row, _ = mods_of(s)
        gate_b = ploc_ref[rs, 0:CV_W].astype(_F32)
        z = ploc_ref[rs, CV_W:2 * CV_W].astype(_F32) * ploc_ref[rs, 2 * CV_W:3 * CV_W].astype(_F32)
        pos = g_row & (GRID_W - 1)
        first = (pos == 0) & (g_row <= seq)
        last = ((pos == GRID_W - 1) & (g_row < seq)) | (g_row == total - 1)
        z_prev = jnp.where(first, 0.0, pltpu.roll(z, 1, 0))
        z_next = jnp.where(last, 0.0, pltpu.roll(z, sb - 1, 0))
        conv = gate_b * (cw_ref[0:1, :] * z_prev + cw_ref[1:2, :] * z + cw_ref[2:3, :] * z_next)
        u = ploc_ref[rs, 3 * CV_W:3 * CV_W + GM_W].astype(_F32)
        v = ploc_ref[rs, 3 * CV_W + GM_W:3 * CV_W + 2 * GM_W].astype(_F32)
        vn = (v * _rms_scale(v) * gmn_ref[...]).astype(_BF16)
        zs = []
        for n in range(sb // CHUNK):
            vc = vn[n * CHUNK:(n + 1) * CHUNK]
            zc = gmb_ref[...]
            for g in range(GM_GROUPS):
                zc = zc + jnp.where(group == g, _dot(gmw_ref[g], vc), 0.0)
            zs.append(zc)
        gm = u * jnp.concatenate(zs, axis=0)
        return jnp.concatenate([ml_ref[rs, :], conv.astype(_BF16), gm.astype(_BF16)], axis=1)

    def proj_stage(s, mix):
        _, mod = mods_of(s)
        x1 = x_ref[rows_of(s), :] + mod(g1_ref) * _dot(mix, wo_ref[...])
        hx2 = x1 * _rms_scale(x1) * n2_ref[...]
        return x1, (hx2 * (1.0 + mod(sc2_ref)) + mod(sh2_ref)).astype(_BF16)

    def up_stage(hx2):
        a = _dot(hx2, w1_ref[...])
        return (a * jax.nn.sigmoid(a) * _dot(hx2, w3_ref[...])).astype(_BF16)

    def down_stage(s, x1, act):
        _, mod = mods_of(s)
        x2 = x1 + mod(g2_ref) * _dot(act, w2_ref[...])
        if final_norm:
            x2 = x2 * _rms_scale(x2) * nf_ref[...]
        o_ref[rows_of(s), :] = x2

    mix, x1, hx2, act = {}, {}, {}, {}
    for k, s in _skewed_order(n_sub, 4):
        if k == 0:
            mix[s] = mix_stage(s)
        elif k == 1:
            x1[s], hx2[s] = proj_stage(s, mix.pop(s))
        elif k == 2:
            act[s] = up_stage(hx2.pop(s))
        else:
            down_stage(s, x1.pop(s), act.pop(s))


def _out_ffn(xs, ml, p_loc, mods, layer, norm2, conv_w, gm_norm, gm_w, gm_b, w_out, w1, w3, w2, norm_f,
             *, seq, out_rows, tile_rows, final_norm):
    bsz, t, _ = xs.shape
    tm = tile_rows
    kern = functools.partial(_out_ffn_kernel, seq=seq, total=t, batch_rows=bsz, final_norm=final_norm)
    row_map = lambda b, i: (b, i, 0)
    mod_spec = lambda k: pl.BlockSpec((None, MOD_ROWS, D_MODEL), lambda b, i: (layer, 0, k))
    lay = lambda *shape: pl.BlockSpec((None,) + shape, lambda b, i: (layer,) + (0,) * len(shape))
    return pl.pallas_call(
        kern,
        grid=(bsz, pl.cdiv(out_rows, tm)),
        in_specs=[
            pl.BlockSpec((None, tm, D_MODEL), row_map),
            pl.BlockSpec((None, tm, ML_W), row_map),
            pl.BlockSpec((None, tm, P_LOC_W), row_map),
            mod_spec(2), mod_spec(3), mod_spec(4), mod_spec(5),
            lay(1, D_MODEL),
            lay(SUBLANES, CV_W),
            lay(1, GM_W),
            lay(GM_GROUPS, CHUNK, CHUNK),
            lay(CHUNK, GM_W),
            lay(D_MODEL, D_MODEL),
            lay(D_MODEL, D_FF),
            lay(D_MODEL, D_FF),
            lay(D_FF, D_MODEL),
            pl.BlockSpec((1, D_MODEL), lambda b, i: (0, 0)),
        ],
        out_specs=pl.BlockSpec((None, tm, D_MODEL), row_map),
        out_shape=jax.ShapeDtypeStruct((bsz, out_rows, D_MODEL), _F32),
        compiler_params=pltpu.CompilerParams(vmem_limit_bytes=_VMEM_LIMIT),
        name="out_ffn",
    )(xs, ml, p_loc, mods, mods, mods, mods, norm2, conv_w, gm_norm, gm_w, gm_b, w_out, w1, w3, w2, norm_f)


def _prepare_in_weights(w_in, b_gates):
    sl = lambda off, width: w_in[:, :, off:off + width]
    w_main = jnp.concatenate(
        [sl(_OFF_V, ML_W), sl(_OFF_Q, ML_W), sl(_OFF_O, ML_W), sl(_OFF_CV, 3 * CV_W), sl(_OFF_GM, 2 * GM_W)],
        axis=-1).astype(_BF16)
    order = (jnp.array([0, 2, 1, 3])[:, None] * ML_HEADS + jnp.arange(ML_HEADS)[None, :]).reshape(-1)
    w_kg = jnp.concatenate([sl(_OFF_K, ML_W), jnp.take(sl(_OFF_G, N_GATES), order, axis=-1)], axis=-1)
    w_kg = jnp.swapaxes(w_kg, 1, 2).astype(_BF16)
    bias = jnp.take(b_gates, order, axis=-1).astype(_F32)
    bias = jnp.broadcast_to(bias[:, :, None], bias.shape + (CHUNK,))
    return w_main, w_kg, bias[:, :N_DIRHEAD], bias[:, N_DIRHEAD:]


def kernel(x, c, ctx, c_ctx, w_ada, b_ada, norm1, norm2, w_in, b_gates, ml_norm, conv_w, gm_norm, gm_ws,
           gm_bs, w_out, w1, w3, w2, norm_f):
    bsz, seq, _ = x.shape
    ctx_len = ctx.shape[1]
    depth = w_in.shape[0]
    total = seq + ctx_len
    assert bsz < MOD_ROWS and w_in.shape[-1] == _D_IN and w1.shape[-1] == D_FF
    assert seq % GRID_W == 0 and seq % CHUNK == 0 and ctx_len % CHUNK == 0
    assert total % TILE_ROWS == 0 and TILE_ROWS % CHUNK == 0 and total - TILE_ROWS <= seq
    assert seq % LAST_TILE_ROWS == 0 and TILE_ROWS % SUB_ROWS == 0 and LAST_TILE_ROWS % SUB_ROWS == 0
    assert seq % SUB_ROWS == 0 and ctx_len <= SUB_ROWS and SUB_ROWS % CHUNK == 0 and SUB_ROWS % GRID_W == 0
    n_ctx_chunks = ctx_len // CHUNK

    cc = jnp.concatenate([c, c_ctx[None, :], jnp.zeros((MOD_ROWS - bsz - 1, D_MODEL), _F32)], axis=0)
    mods = _ada_table(cc, w_ada, b_ada)

    w_main, w_kg, bias_in, bias_fg = _prepare_in_weights(w_in, b_gates)
    wo_b, w1_b, w3_b, w2_b = (w.astype(_BF16) for w in (w_out, w1, w3, w2))
    gm_w_b = gm_ws.astype(_BF16)
    gm_bias = jnp.repeat(jnp.swapaxes(gm_bs, 1, 2), GM_W // GM_GROUPS, axis=2)
    conv_w8 = jnp.pad(conv_w, ((0, 0), (0, SUBLANES - conv_w.shape[1]), (0, 0)))
    norm1_r = norm1.reshape(depth, 1, D_MODEL)
    norm2_r = norm2.reshape(depth, 1, D_MODEL)
    gm_norm_r = gm_norm.reshape(depth, 1, GM_W)
    ml_norm_r = ml_norm.reshape(depth, 1, ML_W)
    norm_f_r = norm_f.reshape(1, D_MODEL)

    xs = jnp.concatenate([x, ctx], axis=1)
    for l in range(depth):
        last = l == depth - 1
        p_ml, p_loc, k_t, g_in, g_fg = _in_proj(xs, mods, l, norm1_r, w_main, w_kg, seq=seq)
        ml = _mlstm(p_ml, k_t, g_in, g_fg, bias_in[l], bias_fg[l], ml_norm_r[l],
                    n_ctx_chunks=n_ctx_chunks, ctx_out=not last)
        xs = _out_ffn(xs, ml, p_loc, mods, l, norm2_r, conv_w8, gm_norm_r, gm_w_b, gm_bias,
                      wo_b, w1_b, w3_b, w2_b, norm_f_r,
                      seq=seq, out_rows=seq if last else total,
                      tile_rows=LAST_TILE_ROWS if last else TILE_ROWS, final_norm=last)
    return xs
```

```python
import functools

import jax
import jax.numpy as jnp
from jax import lax
from jax.experimental import pallas as pl
from jax.experimental.pallas import tpu as pltpu

D_MODEL = 1024
GRID_W = 64
ML_W = D_MODEL // 2
ML_HEADS = 4
ML_DH = ML_W // ML_HEADS
CHUNK = 128
CV_W = D_MODEL // 4
GM_W = D_MODEL // 4
GM_GROUPS = 4
N_GATES = 4 * ML_HEADS
D_FF = 2816
EPS = 1e-6
LOG2_E = 1.4426950408889634
LANES = 128
SUBLANES = 8
N_DIRHEAD = 2 * ML_HEADS

_OFF_K = 0
_OFF_V = _OFF_K + ML_W
_OFF_G = _OFF_V + ML_W
_OFF_Q = _OFF_G + N_GATES
_OFF_O = _OFF_Q + ML_W
_OFF_CV = _OFF_O + ML_W
_OFF_GM = _OFF_CV + 3 * CV_W
_D_IN = _OFF_GM + 2 * GM_W

P_ML_W = 3 * ML_W
P_LOC_W = 3 * CV_W + 2 * GM_W
MIX_LOC_W = CV_W + GM_W
KG_ROWS = ML_W + N_GATES
MOD_ROWS = 24
TILE_ROWS = 768
LAST_TILE_ROWS = 1024
SUB_ROWS = 256

_VMEM_LIMIT = 56 * 1024 * 1024

_BF16 = jnp.bfloat16
_F32 = jnp.float32


def _dot(a, b):
    return jnp.dot(a, b, preferred_element_type=_F32)


def _rms_scale(x):
    return lax.rsqrt(jnp.mean(x * x, axis=-1, keepdims=True) + EPS)


def _ada_kernel(c_ref, w_ref, b_ref, o_ref):
    c = c_ref[...]
    s = (c * jax.nn.sigmoid(c)).astype(_BF16)
    o_ref[...] = _dot(s, w_ref[...].astype(_BF16)) + b_ref[...]


def _ada_table(cc, w_ada, b_ada):
    depth, _, n = w_ada.shape
    tn = 1536
    return pl.pallas_call(
        _ada_kernel,
        grid=(depth, n // tn),
        in_specs=[
            pl.BlockSpec((MOD_ROWS, D_MODEL), lambda l, j: (0, 0)),
            pl.BlockSpec((None, D_MODEL, tn), lambda l, j: (l, 0, j)),
            pl.BlockSpec((None, 1, tn), lambda l, j: (l, 0, j)),
        ],
        out_specs=pl.BlockSpec((None, MOD_ROWS, tn), lambda l, j: (l, 0, j)),
        out_shape=jax.ShapeDtypeStruct((depth, MOD_ROWS, n), _F32),
        compiler_params=pltpu.CompilerParams(vmem_limit_bytes=_VMEM_LIMIT),
        name="ada_table",
    )(cc, w_ada, b_ada.reshape(depth, 1, n))


def _skewed_order(n_sub, n_stages):
    return [(k, t - k) for t in range(n_sub + n_stages - 1) for k in range(n_stages) if 0 <= t - k < n_sub]


def _residual_specs(x_rows, tm, ctx_block):
    n_sub = tm // SUB_ROWS
    n_blocks = x_rows // SUB_ROWS

    def sub_map(b, i, *, s):
        return (b, jnp.minimum(i * n_sub + s, n_blocks - 1), 0)

    subs = [pl.BlockSpec((None, SUB_ROWS, D_MODEL), functools.partial(sub_map, s=s)) for s in range(n_sub)]
    return subs + [pl.BlockSpec((None, SUB_ROWS, D_MODEL), lambda b, i: (b, ctx_block, 0))]


def _residual_sub(x_refs, ctx_ref, s, seq):
    n_sub = len(x_refs)
    is_ctx = (pl.program_id(1) * n_sub + s) * SUB_ROWS >= seq
    return is_ctx, jnp.where(is_ctx, ctx_ref[...], x_refs[s][...])


def _mod_vec(mod_ref, is_ctx, batch_rows):
    return jnp.where(is_ctx, mod_ref[batch_rows:batch_rows + 1, :], mod_ref[pl.ds(pl.program_id(0), 1), :])


def _local_mix(p_loc, is_ctx, cw_ref, gmn_ref, gmw_ref, gmb_ref):
    sb = p_loc.shape[0]
    t_idx = lax.broadcasted_iota(jnp.int32, (sb, 1), 0)
    pos = jnp.where(is_ctx, t_idx, t_idx & (GRID_W - 1))
    last_pos = jnp.where(is_ctx, sb - 1, GRID_W - 1)
    gate_b = p_loc[:, 0:CV_W]
    z = p_loc[:, CV_W:2 * CV_W] * p_loc[:, 2 * CV_W:3 * CV_W]
    z_prev = jnp.where(pos == 0, 0.0, pltpu.roll(z, 1, 0))
    z_next = jnp.where(pos == last_pos, 0.0, pltpu.roll(z, sb - 1, 0))
    conv = gate_b * (cw_ref[0:1, :] * z_prev + cw_ref[1:2, :] * z + cw_ref[2:3, :] * z_next)
    u = p_loc[:, 3 * CV_W:3 * CV_W + GM_W]
    v = p_loc[:, 3 * CV_W + GM_W:3 * CV_W + 2 * GM_W]
    vn = (v * _rms_scale(v) * gmn_ref[...]).astype(_BF16)
    group = lax.broadcasted_iota(jnp.int32, (CHUNK, GM_W), 1) // (GM_W // GM_GROUPS)
    zs = []
    for n in range(sb // CHUNK):
        vc = vn[n * CHUNK:(n + 1) * CHUNK]
        zc = gmb_ref[...]
        for g in range(GM_GROUPS):
            zc = zc + jnp.where(group == g, _dot(gmw_ref[g], vc), 0.0)
        zs.append(zc)
    gm = u * jnp.concatenate(zs, axis=0)
    return jnp.concatenate([conv.astype(_BF16), gm.astype(_BF16)], axis=1)


def _in_proj_kernel(*refs, n_sub, seq, batch_rows):
    x_refs, ctx_ref = refs[:n_sub], refs[n_sub]
    (sh_ref, sc_ref, nw_ref, w_ref, wkg_ref, cw_ref, gmn_ref, gmw_ref, gmb_ref,
     pml_ref, mix_ref, kt_ref, gi_ref, gf_ref) = refs[n_sub + 1:]
    sb = SUB_ROWS
    cps = sb // CHUNK
    k_scale = ML_DH ** -0.5

    def norm_stage(s):
        is_ctx, x = _residual_sub(x_refs, ctx_ref, s, seq)
        xn = x * _rms_scale(x) * nw_ref[...]
        hx = xn * (1.0 + _mod_vec(sc_ref, is_ctx, batch_rows)) + _mod_vec(sh_ref, is_ctx, batch_rows)
        return is_ctx, hx.astype(_BF16)

    def proj_stage(s, hb):
        rs = slice(s * sb, (s + 1) * sb)
        p = _dot(hb, w_ref[...])
        pml_ref[rs, :] = p[:, :P_ML_W].astype(_BF16)
        kg = lax.dot_general(wkg_ref[...], hb, (((1,), (1,)), ((), ())), preferred_element_type=_F32)
        for n in range(cps):
            c = s * cps + n
            lanes = slice(n * CHUNK, (n + 1) * CHUNK)
            kt_ref[c] = (kg[:ML_W, lanes] * k_scale).astype(_BF16)
            gi_ref[c * N_DIRHEAD:(c + 1) * N_DIRHEAD, :] = kg[ML_W:ML_W + N_DIRHEAD, lanes]
            gf_ref[c * N_DIRHEAD:(c + 1) * N_DIRHEAD, :] = kg[ML_W + N_DIRHEAD:, lanes]
        return p[:, P_ML_W:]

    def local_stage(s, is_ctx, p_loc):
        mix_ref[s * sb:(s + 1) * sb, :] = _local_mix(p_loc, is_ctx, cw_ref, gmn_ref, gmw_ref, gmb_ref)

    is_ctx, hb, p_loc = {}, {}, {}
    for k, s in _skewed_order(n_sub, 3):
        if k == 0:
            is_ctx[s], hb[s] = norm_stage(s)
        elif k == 1:
            p_loc[s] = proj_stage(s, hb.pop(s))
        else:
            local_stage(s, is_ctx.pop(s), p_loc.pop(s))


def _in_proj(x_arr, ctx_arr, mods, layer, norm_w, w_main, w_kg, conv_w, gm_norm, gm_w, gm_b,
             *, seq, total, ctx_block):
    bsz = x_arr.shape[0]
    tm = TILE_ROWS
    n_sub = tm // SUB_ROWS
    cpt = tm // CHUNK
    n_chunks = total // CHUNK
    kern = functools.partial(_in_proj_kernel, n_sub=n_sub, seq=seq, batch_rows=bsz)
    row_map = lambda b, i: (b, i, 0)
    lay = lambda *shape: pl.BlockSpec((None,) + shape, lambda b, i: (layer,) + (0,) * len(shape))
    return pl.pallas_call(
        kern,
        grid=(bsz, total // tm),
        in_specs=_residual_specs(x_arr.shape[1], tm, ctx_block) + [
            pl.BlockSpec((None, MOD_ROWS, D_MODEL), lambda b, i: (layer, 0, 0)),
            pl.BlockSpec((None, MOD_ROWS, D_MODEL), lambda b, i: (layer, 0, 1)),
            lay(1, D_MODEL),
            lay(D_MODEL, P_ML_W + P_LOC_W),
            lay(KG_ROWS, D_MODEL),
            lay(SUBLANES, CV_W),
            lay(1, GM_W),
            lay(GM_GROUPS, CHUNK, CHUNK),
            lay(CHUNK, GM_W),
        ],
        out_specs=[
            pl.BlockSpec((None, tm, P_ML_W), row_map),
            pl.BlockSpec((None, tm, MIX_LOC_W), row_map),
            pl.BlockSpec((None, cpt, ML_W, CHUNK), lambda b, i: (b, i, 0, 0)),
            pl.BlockSpec((None, cpt * N_DIRHEAD, CHUNK), row_map),
            pl.BlockSpec((None, cpt * N_DIRHEAD, CHUNK), row_map),
        ],
        out_shape=[
            jax.ShapeDtypeStruct((bsz, total, P_ML_W), _BF16),
            jax.ShapeDtypeStruct((bsz, total, MIX_LOC_W), _BF16),
            jax.ShapeDtypeStruct((bsz, n_chunks, ML_W, CHUNK), _BF16),
            jax.ShapeDtypeStruct((bsz, n_chunks * N_DIRHEAD, CHUNK), _F32),
            jax.ShapeDtypeStruct((bsz, n_chunks * N_DIRHEAD, CHUNK), _F32),
        ],
        compiler_params=pltpu.CompilerParams(vmem_limit_bytes=_VMEM_LIMIT),
        name="in_proj",
    )(*([x_arr] * n_sub), ctx_arr, mods, mods, norm_w, w_main, w_kg, conv_w, gm_norm, gm_w, gm_b)


def _directional_scan(x, combine, fill):
    row = lax.broadcasted_iota(jnp.int32, x.shape, 0)
    lane = lax.broadcasted_iota(jnp.int32, x.shape, 1)
    fwd = (row & (N_DIRHEAD - 1)) < ML_HEADS
    n = x.shape[1]
    s = 1
    while s < n:
        prev = jnp.where(lane >= s, pltpu.roll(x, s, 1), fill)
        nxt = jnp.where(lane < n - s, pltpu.roll(x, n - s, 1), fill)
        x = combine(x, jnp.where(fwd, prev, nxt))
        s *= 2
    return x


def _mlstm_kernel(p_ref, kt_ref, gi_ref, gf_ref, bi_ref, bf_ref, mln_ref, o_ref,
                  r_s, cm_s, b_s, bsum_s, rmax_s, cn_s, rhs_s, kti_s, hf_s, hb_s,
                  *, n_chunks, n_ctx_chunks, ctx_out):
    L = CHUNK
    R = N_DIRHEAD
    n_lat = n_chunks - n_ctx_chunks
    neg_inf = -jnp.inf

    tile_rows = lambda a: jnp.concatenate([a] * n_chunks, axis=0)
    li = gi_ref[...] + tile_rows(bi_ref[...])
    lf = jax.nn.log_sigmoid(gf_ref[...] + tile_rows(bf_ref[...]))
    b = _directional_scan(lf, jnp.add, 0.0)
    r = li - b
    r_s[...] = r
    b_s[...] = b
    cm_s[...] = _directional_scan(r, jnp.maximum, neg_inf)
    bsum_s[...] = jnp.broadcast_to(jnp.sum(lf, axis=1, keepdims=True), lf.shape)
    rmax_s[...] = jnp.broadcast_to(jnp.max(r, axis=1, keepdims=True), lf.shape)

    cn_s[...] = jnp.zeros_like(cn_s)
    eye = (lax.broadcasted_iota(jnp.int32, (ML_DH, ML_DH), 0)
           == lax.broadcasted_iota(jnp.int32, (ML_DH, ML_DH), 1)).astype(_BF16)
    for j in range(R):
        rhs_s[j, 0:L, ML_DH:] = jnp.ones((L, ML_DH), _BF16)
        rhs_s[j, L:, :] = jnp.zeros((ML_DH, 2 * ML_DH), _BF16)
        kti_s[j, :, L:] = eye

    def make_step(with_out):
        def step(i, m_prev):
            cf = jnp.where(i < n_ctx_chunks, n_lat + i, i - n_ctx_chunks)
            cb = n_chunks - 1 - i
            fwd_rows = lax.broadcasted_iota(jnp.int32, (R, L), 0) < ML_HEADS
            rows_f = pl.ds(pl.multiple_of(cf * R, R), R)
            rows_b = pl.ds(pl.multiple_of(cb * R, R), R)
            pick = lambda ref: jnp.where(fwd_rows, ref[rows_f, :], ref[rows_b, :])
            r8 = pick(r_s)
            rmax = pick(rmax_s)
            a8 = jnp.maximum(m_prev, pick(cm_s))
            e_negm = jnp.exp(-(pick(b_s) + a8))
            a_last = jnp.maximum(m_prev, rmax)
            decay = jnp.exp(m_prev - a_last)
            f_kw = jnp.exp(r8 - rmax) * jnp.exp(rmax - a_last)
            m_new = pick(bsum_s) + a_last
            pad_rows = jnp.zeros((LANES - 2 * R, L), _F32)
            cols = jnp.concatenate([a8 * LOG2_E, e_negm, pad_rows], axis=0).T
            r8_l2 = r8 * LOG2_E
            m_prev_l2 = m_prev * LOG2_E
            if with_out:
                t_idx = lax.broadcasted_iota(jnp.int32, (L, L), 0)
                s_idx = lax.broadcasted_iota(jnp.int32, (L, L), 1)
                causal = (s_idx <= t_idx, s_idx >= t_idx)

            chunk_of = lambda j: cf if j < ML_HEADS else cb
            rows_of = lambda j: pl.ds(pl.multiple_of(chunk_of(j) * L, L), L)
            head_of = lambda j: slice((j % ML_HEADS) * ML_DH, (j % ML_HEADS + 1) * ML_DH)
            kts, qk_qs, s_exts, zs, upds = [], [], [], [], []
            for j in range(R):
                kt = kt_ref[chunk_of(j), head_of(j), :]
                kts.append(kt)
                rhs_s[j, 0:L, 0:ML_DH] = p_ref[rows_of(j), head_of(j)]
                if with_out:
                    kti_s[j, :, 0:L] = kt
                    h = j % ML_HEADS
                    q = p_ref[rows_of(j), ML_W + h * ML_DH:ML_W + (h + 1) * ML_DH]
                    qk_qs.append(_dot(q, kti_s[j]))
            if with_out:
                for j in range(R):
                    a_col = jnp.broadcast_to(cols[:, j:j + 1], (L, L))
                    logw = jnp.concatenate(
                        [jnp.where(causal[j // ML_HEADS], r8_l2[j:j + 1, :] - a_col, neg_inf),
                         m_prev_l2[j:j + 1, :] - a_col], axis=1)
                    s_exts.append((qk_qs[j] * jnp.exp2(logw)).astype(_BF16))
                for j in range(R):
                    zs.append(_dot(s_exts[j], rhs_s[j]))
                for j in range(R):
                    z = zs[j]
                    hd_s = hf_s if j < ML_HEADS else hb_s
                    floor = jnp.broadcast_to(cols[:, R + j:R + j + 1], (L, ML_DH))
                    hd_s[rows_of(j), head_of(j)] = z[:, :ML_DH] / jnp.maximum(jnp.abs(z[:, ML_DH:]), floor)
            for j in range(R):
                kw = (kts[j].astype(_F32) * f_kw[j:j + 1, :]).astype(_BF16)
                upds.append(_dot(kw, rhs_s[j, 0:L, :]))
            for j in range(R):
                cn = decay[j:j + 1, 0:1] * cn_s[j] + upds[j]
                cn_s[j] = cn
                rhs_s[j, L:, :] = cn.astype(_BF16)
            return m_new
        return step

    m0 = jnp.zeros((R, L), _F32)
    m1 = lax.fori_loop(0, n_ctx_chunks, make_step(ctx_out), m0)
    lax.fori_loop(n_ctx_chunks, n_chunks, make_step(True), m1)

    def finish(c, carry):
        rows = pl.ds(pl.multiple_of(c * L, L), L)
        for h in range(ML_HEADS):
            hs = slice(h * ML_DH, (h + 1) * ML_DH)
            hh = hf_s[rows, hs] + hb_s[rows, hs]
            hn = hh * _rms_scale(hh) * mln_ref[:, hs]
            o = jax.nn.sigmoid(p_ref[rows, 2 * ML_W + h * ML_DH:2 * ML_W + (h + 1) * ML_DH].astype(_F32))
            o_ref[rows, hs] = (o * hn).astype(_BF16)
        return carry

    lax.fori_loop(0, n_chunks if ctx_out else n_lat, finish, 0)
    if not ctx_out:
        o_ref[n_lat * L:, :] = jnp.zeros((n_ctx_chunks * L, ML_W), _BF16)


def _mlstm(p_ml, k_t, g_in, g_fg, bias_in, bias_fg, ml_norm, *, n_ctx_chunks, ctx_out):
    bsz, t, _ = p_ml.shape
    n_chunks = t // CHUNK
    kern = functools.partial(_mlstm_kernel, n_chunks=n_chunks, n_ctx_chunks=n_ctx_chunks, ctx_out=ctx_out)
    gate_rows = n_chunks * N_DIRHEAD
    gate_scratch = pltpu.VMEM((gate_rows, CHUNK), _F32)
    return pl.pallas_call(
        kern,
        grid=(bsz,),
        in_specs=[
            pl.BlockSpec((None, t, P_ML_W), lambda b: (b, 0, 0)),
            pl.BlockSpec((None, n_chunks, ML_W, CHUNK), lambda b: (b, 0, 0, 0)),
            pl.BlockSpec((None, gate_rows, CHUNK), lambda b: (b, 0, 0)),
            pl.BlockSpec((None, gate_rows, CHUNK), lambda b: (b, 0, 0)),
            pl.BlockSpec((N_DIRHEAD, CHUNK), lambda b: (0, 0)),
            pl.BlockSpec((N_DIRHEAD, CHUNK), lambda b: (0, 0)),
            pl.BlockSpec((1, ML_W), lambda b: (0, 0)),
        ],
        out_specs=pl.BlockSpec((None, t, ML_W), lambda b: (b, 0, 0)),
        out_shape=jax.ShapeDtypeStruct((bsz, t, ML_W), _BF16),
        scratch_shapes=[
            gate_scratch,
            gate_scratch,
            gate_scratch,
            gate_scratch,
            gate_scratch,
            pltpu.VMEM((N_DIRHEAD, ML_DH, 2 * ML_DH), _F32),
            pltpu.VMEM((N_DIRHEAD, CHUNK + ML_DH, 2 * ML_DH), _BF16),
            pltpu.VMEM((N_DIRHEAD, ML_DH, CHUNK + ML_DH), _BF16),
            pltpu.VMEM((t, ML_W), _F32),
            pltpu.VMEM((t, ML_W), _F32),
        ],
        compiler_params=pltpu.CompilerParams(vmem_limit_bytes=_VMEM_LIMIT),
        name="mlstm",
    )(p_ml, k_t, g_in, g_fg, bias_in, bias_fg, ml_norm)


def _out_ffn_kernel(*refs, n_sub, seq, batch_rows, final_norm):
    x_refs, ctx_ref = refs[:n_sub], refs[n_sub]
    (ml_ref, mix_ref, g1_ref, sh2_ref, sc2_ref, g2_ref, n2_ref, wo_ref, w1_ref, w3_ref, w2_ref, nf_ref,
     o_ref) = refs[n_sub + 1:]
    sb = SUB_ROWS

    def proj_stage(s):
        rs = slice(s * sb, (s + 1) * sb)
        is_ctx, x = _residual_sub(x_refs, ctx_ref, s, seq)
        mix = jnp.concatenate([ml_ref[rs, :], mix_ref[rs, :]], axis=1)
        x1 = x + _mod_vec(g1_ref, is_ctx, batch_rows) * _dot(mix, wo_ref[...])
        hx2 = x1 * _rms_scale(x1) * n2_ref[...]
        hx2 = hx2 * (1.0 + _mod_vec(sc2_ref, is_ctx, batch_rows)) + _mod_vec(sh2_ref, is_ctx, batch_rows)
        return is_ctx, x1, hx2.astype(_BF16)

    def up_stage(hx2):
        a = _dot(hx2, w1_ref[...])
        return (a * jax.nn.sigmoid(a) * _dot(hx2, w3_ref[...])).astype(_BF16)

    def down_stage(s, is_ctx, x1, act):
        x2 = x1 + _mod_vec(g2_ref, is_ctx, batch_rows) * _dot(act, w2_ref[...])
        if final_norm:
            x2 = x2 * _rms_scale(x2) * nf_ref[...]
        o_ref[s * sb:(s + 1) * sb, :] = x2

    is_ctx, x1, hx2, act = {}, {}, {}, {}
    for k, s in _skewed_order(n_sub, 3):
        if k == 0:
            is_ctx[s], x1[s], hx2[s] = proj_stage(s)
        elif k == 1:
            act[s] = up_stage(hx2.pop(s))
        else:
            down_stage(s, is_ctx.pop(s), x1.pop(s), act.pop(s))


def _out_ffn(x_arr, ctx_arr, ml, mix_loc, mods, layer, norm2, w_out, w1, w3, w2, norm_f,
             *, seq, ctx_block, out_rows, tile_rows, final_norm):
    bsz = x_arr.shape[0]
    tm = tile_rows
    n_sub = tm // SUB_ROWS
    kern = functools.partial(_out_ffn_kernel, n_sub=n_sub, seq=seq, batch_rows=bsz, final_norm=final_norm)
    row_map = lambda b, i: (b, i, 0)
    mod_spec = lambda k: pl.BlockSpec((None, MOD_ROWS, D_MODEL), lambda b, i: (layer, 0, k))
    lay = lambda *shape: pl.BlockSpec((None,) + shape, lambda b, i: (layer,) + (0,) * len(shape))
    return pl.pallas_call(
        kern,
        grid=(bsz, out_rows // tm),
        in_specs=_residual_specs(x_arr.shape[1], tm, ctx_block) + [
            pl.BlockSpec((None, tm, ML_W), row_map),
            pl.BlockSpec((None, tm, MIX_LOC_W), row_map),
            mod_spec(2), mod_spec(3), mod_spec(4), mod_spec(5),
            lay(1, D_MODEL),
            lay(D_MODEL, D_MODEL),
            lay(D_MODEL, D_FF),
            lay(D_MODEL, D_FF),
            lay(D_FF, D_MODEL),
            pl.BlockSpec((1, D_MODEL), lambda b, i: (0, 0)),
        ],
        out_specs=pl.BlockSpec((None, tm, D_MODEL), row_map),
        out_shape=jax.ShapeDtypeStruct((bsz, out_rows, D_MODEL), _F32),
        compiler_params=pltpu.CompilerParams(vmem_limit_bytes=_VMEM_LIMIT),
        name="out_ffn",
    )(*([x_arr] * n_sub), ctx_arr, ml, mix_loc, mods, mods, mods, mods, norm2, w_out, w1, w3, w2, norm_f)


def _prepare_in_weights(w_in, b_gates):
    sl = lambda off, width: w_in[:, :, off:off + width]
    w_main = jnp.concatenate(
        [sl(_OFF_V, ML_W), sl(_OFF_Q, ML_W), sl(_OFF_O, ML_W), sl(_OFF_CV, 3 * CV_W), sl(_OFF_GM, 2 * GM_W)],
        axis=-1).astype(_BF16)
    order = (jnp.array([0, 2, 1, 3])[:, None] * ML_HEADS + jnp.arange(ML_HEADS)[None, :]).reshape(-1)
    w_kg = jnp.concatenate([sl(_OFF_K, ML_W), jnp.take(sl(_OFF_G, N_GATES), order, axis=-1)], axis=-1)
    w_kg = jnp.swapaxes(w_kg.astype(_BF16), 1, 2)
    bias = jnp.take(b_gates, order, axis=-1).astype(_F32)
    bias = jnp.broadcast_to(bias[:, :, None], bias.shape + (CHUNK,))
    return w_main, w_kg, bias[:, :N_DIRHEAD], bias[:, N_DIRHEAD:]


def kernel(x, c, ctx, c_ctx, w_ada, b_ada, norm1, norm2, w_in, b_gates, ml_norm, conv_w, gm_norm, gm_ws,
           gm_bs, w_out, w1, w3, w2, norm_f):
    bsz, seq, _ = x.shape
    ctx_len = ctx.shape[1]
    depth = w_in.shape[0]
    total = seq + ctx_len
    assert bsz < MOD_ROWS and w_in.shape[-1] == _D_IN and w1.shape[-1] == D_FF
    assert seq % GRID_W == 0 and seq % CHUNK == 0 and ctx_len % CHUNK == 0
    assert total % TILE_ROWS == 0 and TILE_ROWS % CHUNK == 0 and total - TILE_ROWS <= seq
    assert seq % LAST_TILE_ROWS == 0 and TILE_ROWS % SUB_ROWS == 0 and LAST_TILE_ROWS % SUB_ROWS == 0
    assert seq % SUB_ROWS == 0 and ctx_len == SUB_ROWS and SUB_ROWS % CHUNK == 0 and SUB_ROWS % GRID_W == 0
    n_ctx_chunks = ctx_len // CHUNK

    cc = jnp.concatenate([c, c_ctx[None, :], jnp.zeros((MOD_ROWS - bsz - 1, D_MODEL), _F32)], axis=0)
    mods = _ada_table(cc, w_ada, b_ada)

    w_main, w_kg, bias_in, bias_fg = _prepare_in_weights(w_in, b_gates)
    wo_b, w1_b, w3_b, w2_b = (w.astype(_BF16) for w in (w_out, w1, w3, w2))
    gm_w_b = gm_ws.astype(_BF16)
    gm_bias = jnp.repeat(jnp.swapaxes(gm_bs, 1, 2), GM_W // GM_GROUPS, axis=2)
    conv_w8 = jnp.pad(conv_w, ((0, 0), (0, SUBLANES - conv_w.shape[1]), (0, 0)))
    norm1_r = norm1.reshape(depth, 1, D_MODEL)
    norm2_r = norm2.reshape(depth, 1, D_MODEL)
    gm_norm_r = gm_norm.reshape(depth, 1, GM_W)
    ml_norm_r = ml_norm.reshape(depth, 1, ML_W)
    norm_f_r = norm_f.reshape(1, D_MODEL)

    x_arr, ctx_arr, ctx_block = x, ctx, 0
    for l in range(depth):
        last = l == depth - 1
        p_ml, mix_loc, k_t, g_in, g_fg = _in_proj(x_arr, ctx_arr, mods, l, norm1_r, w_main, w_kg, conv_w8,
                                                  gm_norm_r, gm_w_b, gm_bias,
                                                  seq=seq, total=total, ctx_block=ctx_block)
        ml = _mlstm(p_ml, k_t, g_in, g_fg, bias_in[l], bias_fg[l], ml_norm_r[l],
                    n_ctx_chunks=n_ctx_chunks, ctx_out=not last)
        xs = _out_ffn(x_arr, ctx_arr, ml, mix_loc, mods, l, norm2_r, wo_b, w1_b, w3_b, w2_b, norm_f_r,
                      seq=seq, ctx_block=ctx_block, out_rows=seq if last else total,
                      tile_rows=LAST_TILE_ROWS if last else TILE_ROWS, final_norm=last)
        x_arr, ctx_arr, ctx_block = xs, xs, seq // SUB_ROWS
    return xs
```

```python
import functools

import jax
import jax.numpy as jnp
from jax import lax
from jax.experimental import pallas as pl
from jax.experimental.pallas import tpu as pltpu

D_MODEL = 1024
GRID_W = 64
ML_W = D_MODEL // 2
ML_HEADS = 4
ML_DH = ML_W // ML_HEADS
CHUNK = 128
CV_W = D_MODEL // 4
GM_W = D_MODEL // 4
GM_GROUPS = 4
N_GATES = 4 * ML_HEADS
D_FF = 2816
EPS = 1e-6
LOG2_E = 1.4426950408889634
LANES = 128
SUBLANES = 8
N_DIRHEAD = 2 * ML_HEADS
FINISH_CHUNKS = 2

_OFF_K = 0
_OFF_V = _OFF_K + ML_W
_OFF_G = _OFF_V + ML_W
_OFF_Q = _OFF_G + N_GATES
_OFF_O = _OFF_Q + ML_W
_OFF_CV = _OFF_O + ML_W
_OFF_GM = _OFF_CV + 3 * CV_W
_D_IN = _OFF_GM + 2 * GM_W

P_ML_W = 3 * ML_W
P_LOC_W = 3 * CV_W + 2 * GM_W
MIX_LOC_W = CV_W + GM_W
KG_ROWS = ML_W + N_GATES
MOD_ROWS = 24
TILE_ROWS = 768
LAST_TILE_ROWS = 1024
SUB_ROWS = 256

_VMEM_LIMIT = 56 * 1024 * 1024

_BF16 = jnp.bfloat16
_F32 = jnp.float32


def _dot(a, b):
    return jnp.dot(a, b, preferred_element_type=_F32)


def _rms_scale(x):
    return lax.rsqrt(jnp.mean(x * x, axis=-1, keepdims=True) + EPS)


def _ada_kernel(c_ref, w_ref, b_ref, o_ref):
    c = c_ref[...]
    s = (c * jax.nn.sigmoid(c)).astype(_BF16)
    o_ref[...] = _dot(s, w_ref[...].astype(_BF16)) + b_ref[...]


def _ada_table(cc, w_ada, b_ada):
    depth, _, n = w_ada.shape
    tn = 1536
    return pl.pallas_call(
        _ada_kernel,
        grid=(depth, n // tn),
        in_specs=[
            pl.BlockSpec((MOD_ROWS, D_MODEL), lambda l, j: (0, 0)),
            pl.BlockSpec((None, D_MODEL, tn), lambda l, j: (l, 0, j)),
            pl.BlockSpec((None, 1, tn), lambda l, j: (l, 0, j)),
        ],
        out_specs=pl.BlockSpec((None, MOD_ROWS, tn), lambda l, j: (l, 0, j)),
        out_shape=jax.ShapeDtypeStruct((depth, MOD_ROWS, n), _F32),
        compiler_params=pltpu.CompilerParams(vmem_limit_bytes=_VMEM_LIMIT),
        name="ada_table",
    )(cc, w_ada, b_ada.reshape(depth, 1, n))


def _skewed_order(n_sub, n_stages):
    return [(k, t - k) for t in range(n_sub + n_stages - 1) for k in range(n_stages) if 0 <= t - k < n_sub]


def _residual_specs(x_rows, tm, ctx_block):
    n_sub = tm // SUB_ROWS
    n_blocks = x_rows // SUB_ROWS

    def sub_map(b, i, *, s):
        return (b, jnp.minimum(i * n_sub + s, n_blocks - 1), 0)

    subs = [pl.BlockSpec((None, SUB_ROWS, D_MODEL), functools.partial(sub_map, s=s)) for s in range(n_sub)]
    return subs + [pl.BlockSpec((None, SUB_ROWS, D_MODEL), lambda b, i: (b, ctx_block, 0))]


def _residual_sub(x_refs, ctx_ref, s, seq):
    n_sub = len(x_refs)
    is_ctx = (pl.program_id(1) * n_sub + s) * SUB_ROWS >= seq
    return is_ctx, jnp.where(is_ctx, ctx_ref[...], x_refs[s][...])


def _mod_vec(mod_ref, is_ctx, batch_rows):
    return jnp.where(is_ctx, mod_ref[batch_rows:batch_rows + 1, :], mod_ref[pl.ds(pl.program_id(0), 1), :])


def _local_mix(p_loc, is_ctx, cw_ref, gmn_ref, gmw_ref, gmb_ref):
    sb = p_loc.shape[0]
    t_idx = lax.broadcasted_iota(jnp.int32, (sb, 1), 0)
    pos = jnp.where(is_ctx, t_idx, t_idx & (GRID_W - 1))
    last_pos = jnp.where(is_ctx, sb - 1, GRID_W - 1)
    gate_b = p_loc[:, 0:CV_W]
    z = p_loc[:, CV_W:2 * CV_W] * p_loc[:, 2 * CV_W:3 * CV_W]
    z_prev = jnp.where(pos == 0, 0.0, pltpu.roll(z, 1, 0))
    z_next = jnp.where(pos == last_pos, 0.0, pltpu.roll(z, sb - 1, 0))
    conv = gate_b * (cw_ref[0:1, :] * z_prev + cw_ref[1:2, :] * z + cw_ref[2:3, :] * z_next)
    u = p_loc[:, 3 * CV_W:3 * CV_W + GM_W]
    v = p_loc[:, 3 * CV_W + GM_W:3 * CV_W + 2 * GM_W]
    vn = (v * _rms_scale(v) * gmn_ref[...]).astype(_BF16)
    group = lax.broadcasted_iota(jnp.int32, (CHUNK, GM_W), 1) // (GM_W // GM_GROUPS)
    zero = jnp.zeros((CHUNK, GM_W), _BF16)
    zs = []
    for n in range(sb // CHUNK):
        vc = vn[n * CHUNK:(n + 1) * CHUNK]
        v_groups = jnp.concatenate([jnp.where(group == g, vc, zero) for g in range(GM_GROUPS)], axis=0)
        zs.append(gmb_ref[...] + _dot(gmw_ref[...], v_groups))
    gm = u * jnp.concatenate(zs, axis=0)
    return jnp.concatenate([conv.astype(_BF16), gm.astype(_BF16)], axis=1)


def _in_proj_kernel(*refs, n_sub, seq, batch_rows):
    x_refs, ctx_ref = refs[:n_sub], refs[n_sub]
    (sh_ref, sc_ref, nw_ref, w_ref, wkg_ref, cw_ref, gmn_ref, gmw_ref, gmb_ref,
     pml_ref, mix_ref, kt_ref, gi_ref, gf_ref) = refs[n_sub + 1:]
    sb = SUB_ROWS
    cps = sb // CHUNK
    k_scale = ML_DH ** -0.5

    def norm_stage(s):
        is_ctx, x = _residual_sub(x_refs, ctx_ref, s, seq)
        xn = x * _rms_scale(x) * nw_ref[...]
        hx = xn * (1.0 + _mod_vec(sc_ref, is_ctx, batch_rows)) + _mod_vec(sh_ref, is_ctx, batch_rows)
        return is_ctx, hx.astype(_BF16)

    def proj_stage(s, hb):
        rs = slice(s * sb, (s + 1) * sb)
        p = _dot(hb, w_ref[...])
        pml_ref[rs, :] = p[:, :P_ML_W].astype(_BF16)
        kg = lax.dot_general(wkg_ref[...], hb, (((1,), (1,)), ((), ())), preferred_element_type=_F32)
        for n in range(cps):
            c = s * cps + n
            lanes = slice(n * CHUNK, (n + 1) * CHUNK)
            kt_ref[c] = (kg[:ML_W, lanes] * k_scale).astype(_BF16)
            gi_ref[c * N_DIRHEAD:(c + 1) * N_DIRHEAD, :] = kg[ML_W:ML_W + N_DIRHEAD, lanes]
            gf_ref[c * N_DIRHEAD:(c + 1) * N_DIRHEAD, :] = kg[ML_W + N_DIRHEAD:, lanes]
        return p[:, P_ML_W:]

    def local_stage(s, is_ctx, p_loc):
        mix_ref[s * sb:(s + 1) * sb, :] = _local_mix(p_loc, is_ctx, cw_ref, gmn_ref, gmw_ref, gmb_ref)

    is_ctx, hb, p_loc = {}, {}, {}
    for k, s in _skewed_order(n_sub, 3):
        if k == 0:
            is_ctx[s], hb[s] = norm_stage(s)
        elif k == 1:
            p_loc[s] = proj_stage(s, hb.pop(s))
        else:
            local_stage(s, is_ctx.pop(s), p_loc.pop(s))


def _in_proj(x_arr, ctx_arr, mods, layer, norm_w, w_main, w_kg, conv_w, gm_norm, gm_w, gm_b,
             *, seq, total, ctx_block):
    bsz = x_arr.shape[0]
    tm = TILE_ROWS
    n_sub = tm // SUB_ROWS
    cpt = tm // CHUNK
    n_chunks = total // CHUNK
    kern = functools.partial(_in_proj_kernel, n_sub=n_sub, seq=seq, batch_rows=bsz)
    row_map = lambda b, i: (b, i, 0)
    lay = lambda *shape: pl.BlockSpec((None,) + shape, lambda b, i: (layer,) + (0,) * len(shape))
    return pl.pallas_call(
        kern,
        grid=(bsz, total // tm),
        in_specs=_residual_specs(x_arr.shape[1], tm, ctx_block) + [
            pl.BlockSpec((None, MOD_ROWS, D_MODEL), lambda b, i: (layer, 0, 0)),
            pl.BlockSpec((None, MOD_ROWS, D_MODEL), lambda b, i: (layer, 0, 1)),
            lay(1, D_MODEL),
            lay(D_MODEL, P_ML_W + P_LOC_W),
            lay(KG_ROWS, D_MODEL),
            lay(SUBLANES, CV_W),
            lay(1, GM_W),
            lay(CHUNK, GM_GROUPS * CHUNK),
            lay(CHUNK, GM_W),
        ],
        out_specs=[
            pl.BlockSpec((None, tm, P_ML_W), row_map),
            pl.BlockSpec((None, tm, MIX_LOC_W), row_map),
            pl.BlockSpec((None, cpt, ML_W, CHUNK), lambda b, i: (b, i, 0, 0)),
            pl.BlockSpec((None, cpt * N_DIRHEAD, CHUNK), row_map),
            pl.BlockSpec((None, cpt * N_DIRHEAD, CHUNK), row_map),
        ],
        out_shape=[
            jax.ShapeDtypeStruct((bsz, total, P_ML_W), _BF16),
            jax.ShapeDtypeStruct((bsz, total, MIX_LOC_W), _BF16),
            jax.ShapeDtypeStruct((bsz, n_chunks, ML_W, CHUNK), _BF16),
            jax.ShapeDtypeStruct((bsz, n_chunks * N_DIRHEAD, CHUNK), _F32),
            jax.ShapeDtypeStruct((bsz, n_chunks * N_DIRHEAD, CHUNK), _F32),
        ],
        compiler_params=pltpu.CompilerParams(vmem_limit_bytes=_VMEM_LIMIT),
        name="in_proj",
    )(*([x_arr] * n_sub), ctx_arr, mods, mods, norm_w, w_main, w_kg, conv_w, gm_norm, gm_w, gm_b)


def _directional_scan(x, combine, fill):
    row = lax.broadcasted_iota(jnp.int32, x.shape, 0)
    lane = lax.broadcasted_iota(jnp.int32, x.shape, 1)
    fwd = (row & (N_DIRHEAD - 1)) < ML_HEADS
    n = x.shape[1]
    s = 1
    while s < n:
        prev = jnp.where(lane >= s, pltpu.roll(x, s, 1), fill)
        nxt = jnp.where(lane < n - s, pltpu.roll(x, n - s, 1), fill)
        x = combine(x, jnp.where(fwd, prev, nxt))
        s *= 2
    return x


def _mlstm_kernel(p_ref, kt_ref, gi_ref, gf_ref, bi_ref, bf_ref, mln_ref, wo_ref, o_ref,
                  r_s, cm_s, b_s, bsum_s, rmax_s, cn_s, rhs_s, kti_s, hf_s, hb_s,
                  *, n_chunks, n_ctx_chunks, ctx_out):
    L = CHUNK
    R = N_DIRHEAD
    n_lat = n_chunks - n_ctx_chunks
    neg_inf = -jnp.inf

    tile_rows = lambda a: jnp.concatenate([a] * n_chunks, axis=0)
    li = gi_ref[...] + tile_rows(bi_ref[...])
    lf = jax.nn.log_sigmoid(gf_ref[...] + tile_rows(bf_ref[...]))
    b = _directional_scan(lf, jnp.add, 0.0)
    r = li - b
    r_s[...] = r
    b_s[...] = b
    cm_s[...] = _directional_scan(r, jnp.maximum, neg_inf)
    bsum_s[...] = jnp.broadcast_to(jnp.sum(lf, axis=1, keepdims=True), lf.shape)
    rmax_s[...] = jnp.broadcast_to(jnp.max(r, axis=1, keepdims=True), lf.shape)

    cn_s[...] = jnp.zeros_like(cn_s)
    eye = (lax.broadcasted_iota(jnp.int32, (ML_DH, ML_DH), 0)
           == lax.broadcasted_iota(jnp.int32, (ML_DH, ML_DH), 1)).astype(_BF16)
    for j in range(R):
        rhs_s[j, 0:L, ML_DH:] = jnp.ones((L, ML_DH), _BF16)
        rhs_s[j, L:, :] = jnp.zeros((ML_DH, 2 * ML_DH), _BF16)
        kti_s[j, :, L:] = eye

    def make_step(with_out):
        def step(i, m_prev):
            cf = jnp.where(i < n_ctx_chunks, n_lat + i, i - n_ctx_chunks)
            cb = n_chunks - 1 - i
            fwd_rows = lax.broadcasted_iota(jnp.int32, (R, L), 0) < ML_HEADS
            rows_f = pl.ds(pl.multiple_of(cf * R, R), R)
            rows_b = pl.ds(pl.multiple_of(cb * R, R), R)
            pick = lambda ref: jnp.where(fwd_rows, ref[rows_f, :], ref[rows_b, :])
            r8 = pick(r_s)
            rmax = pick(rmax_s)
            a8 = jnp.maximum(m_prev, pick(cm_s))
            e_negm = jnp.exp(-(pick(b_s) + a8))
            a_last = jnp.maximum(m_prev, rmax)
            decay = jnp.exp(m_prev - a_last)
            f_kw = jnp.exp(r8 - rmax) * jnp.exp(rmax - a_last)
            m_new = pick(bsum_s) + a_last
            pad_rows = jnp.zeros((LANES - 2 * R, L), _F32)
            cols = jnp.concatenate([a8 * LOG2_E, e_negm, pad_rows], axis=0).T
            r8_l2 = r8 * LOG2_E
            m_prev_l2 = m_prev * LOG2_E
            if with_out:
                t_idx = lax.broadcasted_iota(jnp.int32, (L, L), 0)
                s_idx = lax.broadcasted_iota(jnp.int32, (L, L), 1)
                causal = (s_idx <= t_idx, s_idx >= t_idx)

            chunk_of = lambda j: cf if j < ML_HEADS else cb
            rows_of = lambda j: pl.ds(pl.multiple_of(chunk_of(j) * L, L), L)
            head_of = lambda j: slice((j % ML_HEADS) * ML_DH, (j % ML_HEADS + 1) * ML_DH)
            kts, qk_qs, s_exts, zs, upds = [], [], [], [], []
            for j in range(R):
                kt = kt_ref[chunk_of(j), head_of(j), :]
                kts.append(kt)
                rhs_s[j, 0:L, 0:ML_DH] = p_ref[rows_of(j), head_of(j)]
                if with_out:
                    kti_s[j, :, 0:L] = kt
                    h = j % ML_HEADS
                    q = p_ref[rows_of(j), ML_W + h * ML_DH:ML_W + (h + 1) * ML_DH]
                    qk_qs.append(_dot(q, kti_s[j]))
            if with_out:
                for j in range(R):
                    a_col = jnp.broadcast_to(cols[:, j:j + 1], (L, L))
                    logw = jnp.concatenate(
                        [jnp.where(causal[j // ML_HEADS], r8_l2[j:j + 1, :] - a_col, neg_inf),
                         m_prev_l2[j:j + 1, :] - a_col], axis=1)
                    s_exts.append((qk_qs[j] * jnp.exp2(logw)).astype(_BF16))
                for j in range(R):
                    zs.append(_dot(s_exts[j], rhs_s[j]))
                for j in range(R):
                    z = zs[j]
                    hd_s = hf_s if j < ML_HEADS else hb_s
                    floor = jnp.broadcast_to(cols[:, R + j:R + j + 1], (L, ML_DH))
                    hd_s[rows_of(j), head_of(j)] = z[:, :ML_DH] / jnp.maximum(jnp.abs(z[:, ML_DH:]), floor)
            for j in range(R):
                kw = (kts[j].astype(_F32) * f_kw[j:j + 1, :]).astype(_BF16)
                upds.append(_dot(kw, rhs_s[j, 0:L, :]))
            for j in range(R):
                cn = decay[j:j + 1, 0:1] * cn_s[j] + upds[j]
                cn_s[j] = cn
                rhs_s[j, L:, :] = cn.astype(_BF16)
            return m_new
        return step

    m0 = jnp.zeros((R, L), _F32)
    m1 = lax.fori_loop(0, n_ctx_chunks, make_step(ctx_out), m0)
    lax.fori_loop(n_ctx_chunks, n_chunks, make_step(True), m1)

    fin = FINISH_CHUNKS * L

    def gate_stage(k):
        rows = slice(k * fin, (k + 1) * fin)
        heads = []
        for h in range(ML_HEADS):
            hs = slice(h * ML_DH, (h + 1) * ML_DH)
            hh = hf_s[rows, hs] + hb_s[rows, hs]
            hn = hh * _rms_scale(hh) * mln_ref[:, hs]
            og = p_ref[rows, 2 * ML_W + h * ML_DH:2 * ML_W + (h + 1) * ML_DH].astype(_F32)
            o = 0.5 * jnp.tanh(0.5 * og) + 0.5
            heads.append((o * hn).astype(_BF16))
        return jnp.concatenate(heads, axis=1)

    def proj_stage(k, gated):
        o_ref[k * fin:(k + 1) * fin, :] = _dot(gated, wo_ref[...]).astype(_BF16)

    assert n_chunks % FINISH_CHUNKS == 0 and n_lat % FINISH_CHUNKS == 0
    gated = {}
    for stage, k in _skewed_order((n_chunks if ctx_out else n_lat) // FINISH_CHUNKS, 2):
        if stage == 0:
            gated[k] = gate_stage(k)
        else:
            proj_stage(k, gated.pop(k))
    if not ctx_out:
        o_ref[n_lat * L:, :] = jnp.zeros((n_ctx_chunks * L, D_MODEL), _BF16)


def _mlstm(p_ml, k_t, g_in, g_fg, bias_in, bias_fg, ml_norm, w_out_ml, *, n_ctx_chunks, ctx_out):
    bsz, t, _ = p_ml.shape
    n_chunks = t // CHUNK
    kern = functools.partial(_mlstm_kernel, n_chunks=n_chunks, n_ctx_chunks=n_ctx_chunks, ctx_out=ctx_out)
    gate_rows = n_chunks * N_DIRHEAD
    gate_scratch = pltpu.VMEM((gate_rows, CHUNK), _F32)
    return pl.pallas_call(
        kern,
        grid=(bsz,),
        in_specs=[
            pl.BlockSpec((None, t, P_ML_W), lambda b: (b, 0, 0)),
            pl.BlockSpec((None, n_chunks, ML_W, CHUNK), lambda b: (b, 0, 0, 0)),
            pl.BlockSpec((None, gate_rows, CHUNK), lambda b: (b, 0, 0)),
            pl.BlockSpec((None, gate_rows, CHUNK), lambda b: (b, 0, 0)),
            pl.BlockSpec((N_DIRHEAD, CHUNK), lambda b: (0, 0)),
            pl.BlockSpec((N_DIRHEAD, CHUNK), lambda b: (0, 0)),
            pl.BlockSpec((1, ML_W), lambda b: (0, 0)),
            pl.BlockSpec((ML_W, D_MODEL), lambda b: (0, 0)),
        ],
        out_specs=pl.BlockSpec((None, t, D_MODEL), lambda b: (b, 0, 0)),
        out_shape=jax.ShapeDtypeStruct((bsz, t, D_MODEL), _BF16),
        scratch_shapes=[
            gate_scratch,
            gate_scratch,
            gate_scratch,
            gate_scratch,
            gate_scratch,
            pltpu.VMEM((N_DIRHEAD, ML_DH, 2 * ML_DH), _F32),
            pltpu.VMEM((N_DIRHEAD, CHUNK + ML_DH, 2 * ML_DH), _BF16),
            pltpu.VMEM((N_DIRHEAD, ML_DH, CHUNK + ML_DH), _BF16),
            pltpu.VMEM((t, ML_W), _F32),
            pltpu.VMEM((t, ML_W), _F32),
        ],
        compiler_params=pltpu.CompilerParams(vmem_limit_bytes=_VMEM_LIMIT),
        name="mlstm",
    )(p_ml, k_t, g_in, g_fg, bias_in, bias_fg, ml_norm, w_out_ml)


def _out_ffn_kernel(*refs, n_sub, seq, batch_rows, final_norm):
    x_refs, ctx_ref = refs[:n_sub], refs[n_sub]
    (ml_ref, mix_ref, g1_ref, sh2_ref, sc2_ref, g2_ref, n2_ref, wo_ref, w1_ref, w3_ref, w2_ref, nf_ref,
     o_ref) = refs[n_sub + 1:]
    sb = SUB_ROWS

    def proj_stage(s):
        rs = slice(s * sb, (s + 1) * sb)
        is_ctx, x = _residual_sub(x_refs, ctx_ref, s, seq)
        proj = ml_ref[rs, :].astype(_F32) + _dot(mix_ref[rs, :], wo_ref[...])
        x1 = x + _mod_vec(g1_ref, is_ctx, batch_rows) * proj
        hx2 = x1 * _rms_scale(x1) * n2_ref[...]
        hx2 = hx2 * (1.0 + _mod_vec(sc2_ref, is_ctx, batch_rows)) + _mod_vec(sh2_ref, is_ctx, batch_rows)
        return is_ctx, x1, hx2.astype(_BF16)

    def up_stage(hx2):
        a = _dot(hx2, w1_ref[...])
        return (a * jax.nn.sigmoid(a) * _dot(hx2, w3_ref[...])).astype(_BF16)

    def down_stage(s, is_ctx, x1, act):
        x2 = x1 + _mod_vec(g2_ref, is_ctx, batch_rows) * _dot(act, w2_ref[...])
        if final_norm:
            x2 = x2 * _rms_scale(x2) * nf_ref[...]
        o_ref[s * sb:(s + 1) * sb, :] = x2

    is_ctx, x1, hx2, act = {}, {}, {}, {}
    for k, s in _skewed_order(n_sub, 3):
        if k == 0:
            is_ctx[s], x1[s], hx2[s] = proj_stage(s)
        elif k == 1:
            act[s] = up_stage(hx2.pop(s))
        else:
            down_stage(s, is_ctx.pop(s), x1.pop(s), act.pop(s))


def _out_ffn(x_arr, ctx_arr, ml, mix_loc, mods, layer, norm2, w_out, w1, w3, w2, norm_f,
             *, seq, ctx_block, out_rows, tile_rows, final_norm):
    bsz = x_arr.shape[0]
    tm = tile_rows
    n_sub = tm // SUB_ROWS
    kern = functools.partial(_out_ffn_kernel, n_sub=n_sub, seq=seq, batch_rows=bsz, final_norm=final_norm)
    row_map = lambda b, i: (b, i, 0)
    mod_spec = lambda k: pl.BlockSpec((None, MOD_ROWS, D_MODEL), lambda b, i: (layer, 0, k))
    lay = lambda *shape: pl.BlockSpec((None,) + shape, lambda b, i: (layer,) + (0,) * len(shape))
    return pl.pallas_call(
        kern,
        grid=(bsz, out_rows // tm),
        in_specs=_residual_specs(x_arr.shape[1], tm, ctx_block) + [
            pl.BlockSpec((None, tm, D_MODEL), row_map),
            pl.BlockSpec((None, tm, MIX_LOC_W), row_map),
            mod_spec(2), mod_spec(3), mod_spec(4), mod_spec(5),
            lay(1, D_MODEL),
            lay(MIX_LOC_W, D_MODEL),
            lay(D_MODEL, D_FF),
            lay(D_MODEL, D_FF),
            lay(D_FF, D_MODEL),
            pl.BlockSpec((1, D_MODEL), lambda b, i: (0, 0)),
        ],
        out_specs=pl.BlockSpec((None, tm, D_MODEL), row_map),
        out_shape=jax.ShapeDtypeStruct((bsz, out_rows, D_MODEL), _F32),
        compiler_params=pltpu.CompilerParams(vmem_limit_bytes=_VMEM_LIMIT),
        name="out_ffn",
    )(*([x_arr] * n_sub), ctx_arr, ml, mix_loc, mods, mods, mods, mods, norm2, w_out, w1, w3, w2, norm_f)


def _prepare_in_weights(w_in, b_gates):
    sl = lambda off, width: w_in[:, :, off:off + width]
    w_main = jnp.concatenate(
        [sl(_OFF_V, ML_W), sl(_OFF_Q, ML_W), sl(_OFF_O, ML_W), sl(_OFF_CV, 3 * CV_W), sl(_OFF_GM, 2 * GM_W)],
        axis=-1).astype(_BF16)
    order = (jnp.array([0, 2, 1, 3])[:, None] * ML_HEADS + jnp.arange(ML_HEADS)[None, :]).reshape(-1)
    w_kg = jnp.concatenate([sl(_OFF_K, ML_W), jnp.take(sl(_OFF_G, N_GATES), order, axis=-1)], axis=-1)
    w_kg = jnp.swapaxes(w_kg.astype(_BF16), 1, 2)
    bias = jnp.take(b_gates, order, axis=-1).astype(_F32)
    bias = jnp.broadcast_to(bias[:, :, None], bias.shape + (CHUNK,))
    return w_main, w_kg, bias[:, :N_DIRHEAD], bias[:, N_DIRHEAD:]


def kernel(x, c, ctx, c_ctx, w_ada, b_ada, norm1, norm2, w_in, b_gates, ml_norm, conv_w, gm_norm, gm_ws,
           gm_bs, w_out, w1, w3, w2, norm_f):
    bsz, seq, _ = x.shape
    ctx_len = ctx.shape[1]
    depth = w_in.shape[0]
    total = seq + ctx_len
    assert bsz < MOD_ROWS and w_in.shape[-1] == _D_IN and w1.shape[-1] == D_FF
    assert seq % GRID_W == 0 and seq % CHUNK == 0 and ctx_len % CHUNK == 0
    assert total % TILE_ROWS == 0 and TILE_ROWS % CHUNK == 0 and total - TILE_ROWS <= seq
    assert seq % LAST_TILE_ROWS == 0 and TILE_ROWS % SUB_ROWS == 0 and LAST_TILE_ROWS % SUB_ROWS == 0
    assert seq % SUB_ROWS == 0 and ctx_len == SUB_ROWS and SUB_ROWS % CHUNK == 0 and SUB_ROWS % GRID_W == 0
    n_ctx_chunks = ctx_len // CHUNK

    cc = jnp.concatenate([c, c_ctx[None, :], jnp.zeros((MOD_ROWS - bsz - 1, D_MODEL), _F32)], axis=0)
    mods = _ada_table(cc, w_ada, b_ada)

    w_main, w_kg, bias_in, bias_fg = _prepare_in_weights(w_in, b_gates)
    wo_b, w1_b, w3_b, w2_b = (w.astype(_BF16) for w in (w_out, w1, w3, w2))
    gm_w_b = jnp.swapaxes(gm_ws, 1, 2).reshape(depth, CHUNK, GM_GROUPS * CHUNK).astype(_BF16)
    gm_bias = jnp.repeat(jnp.swapaxes(gm_bs, 1, 2), GM_W // GM_GROUPS, axis=2)
    conv_w8 = jnp.pad(conv_w, ((0, 0), (0, SUBLANES - conv_w.shape[1]), (0, 0)))
    norm1_r = norm1.reshape(depth, 1, D_MODEL)
    norm2_r = norm2.reshape(depth, 1, D_MODEL)
    gm_norm_r = gm_norm.reshape(depth, 1, GM_W)
    ml_norm_r = ml_norm.reshape(depth, 1, ML_W)
    norm_f_r = norm_f.reshape(1, D_MODEL)

    x_arr, ctx_arr, ctx_block = x, ctx, 0
    for l in range(depth):
        last = l == depth - 1
        p_ml, mix_loc, k_t, g_in, g_fg = _in_proj(x_arr, ctx_arr, mods, l, norm1_r, w_main, w_kg, conv_w8,
                                                  gm_norm_r, gm_w_b, gm_bias,
                                                  seq=seq, total=total, ctx_block=ctx_block)
        ml = _mlstm(p_ml, k_t, g_in, g_fg, bias_in[l], bias_fg[l], ml_norm_r[l], wo_b[l, :ML_W],
                    n_ctx_chunks=n_ctx_chunks, ctx_out=not last)
        xs = _out_ffn(x_arr, ctx_arr, ml, mix_loc, mods, l, norm2_r, wo_b[:, ML_W:], w1_b, w3_b, w2_b, norm_f_r,
                      seq=seq, ctx_block=ctx_block, out_rows=seq if last else total,
                      tile_rows=LAST_TILE_ROWS if last else TILE_ROWS, final_norm=last)
        x_arr, ctx_arr, ctx_block = xs, xs, seq // SUB_ROWS
    return xs
```

```python
import functools

import jax
import jax.numpy as jnp
from jax import lax
from jax.experimental import pallas as pl
from jax.experimental.pallas import tpu as pltpu

D_MODEL = 1024
GRID_W = 64
ML_W = D_MODEL // 2
ML_HEADS = 4
ML_DH = ML_W // ML_HEADS
CHUNK = 128
CV_W = D_MODEL // 4
GM_W = D_MODEL // 4
GM_GROUPS = 4
N_GATES = 4 * ML_HEADS
D_FF = 2816
EPS = 1e-6
LOG2_E = 1.4426950408889634
LANES = 128
SUBLANES = 8
N_DIRHEAD = 2 * ML_HEADS
FINISH_CHUNKS = 2

_OFF_K = 0
_OFF_V = _OFF_K + ML_W
_OFF_G = _OFF_V + ML_W
_OFF_Q = _OFF_G + N_GATES
_OFF_O = _OFF_Q + ML_W
_OFF_CV = _OFF_O + ML_W
_OFF_GM = _OFF_CV + 3 * CV_W
_D_IN = _OFF_GM + 2 * GM_W

P_ML_W = 3 * ML_W
P_LOC_W = 3 * CV_W + 2 * GM_W
MIX_LOC_W = CV_W + GM_W
KG_ROWS = ML_W + N_GATES
MOD_ROWS = 24
TILE_ROWS = 768
LAST_TILE_ROWS = 1024
SUB_ROWS = 256

_VMEM_LIMIT = 56 * 1024 * 1024

_BF16 = jnp.bfloat16
_F32 = jnp.float32


def _dot(a, b):
    return jnp.dot(a, b, preferred_element_type=_F32)


def _rms_scale(x):
    return lax.rsqrt(jnp.mean(x * x, axis=-1, keepdims=True) + EPS)


def _ada_kernel(c_ref, w_ref, b_ref, o_ref):
    c = c_ref[...]
    s = (c * jax.nn.sigmoid(c)).astype(_BF16)
    o_ref[...] = _dot(s, w_ref[...].astype(_BF16)) + b_ref[...]


def _ada_table(cc, w_ada, b_ada):
    depth, _, n = w_ada.shape
    tn = 1536
    return pl.pallas_call(
        _ada_kernel,
        grid=(depth, n // tn),
        in_specs=[
            pl.BlockSpec((MOD_ROWS, D_MODEL), lambda l, j: (0, 0)),
            pl.BlockSpec((None, D_MODEL, tn), lambda l, j: (l, 0, j)),
            pl.BlockSpec((None, 1, tn), lambda l, j: (l, 0, j)),
        ],
        out_specs=pl.BlockSpec((None, MOD_ROWS, tn), lambda l, j: (l, 0, j)),
        out_shape=jax.ShapeDtypeStruct((depth, MOD_ROWS, n), _F32),
        compiler_params=pltpu.CompilerParams(vmem_limit_bytes=_VMEM_LIMIT),
        name="ada_table",
    )(cc, w_ada, b_ada.reshape(depth, 1, n))


def _skewed_order(n_sub, n_stages):
    return [(k, t - k) for t in range(n_sub + n_stages - 1) for k in range(n_stages) if 0 <= t - k < n_sub]


def _residual_specs(x_rows, tm, ctx_block):
    n_sub = tm // SUB_ROWS
    n_blocks = x_rows // SUB_ROWS

    def sub_map(b, i, *, s):
        return (b, jnp.minimum(i * n_sub + s, n_blocks - 1), 0)

    subs = [pl.BlockSpec((None, SUB_ROWS, D_MODEL), functools.partial(sub_map, s=s)) for s in range(n_sub)]
    return subs + [pl.BlockSpec((None, SUB_ROWS, D_MODEL), lambda b, i: (b, ctx_block, 0))]


def _residual_sub(x_refs, ctx_ref, s, seq):
    n_sub = len(x_refs)
    is_ctx = (pl.program_id(1) * n_sub + s) * SUB_ROWS >= seq
    return is_ctx, jnp.where(is_ctx, ctx_ref[...], x_refs[s][...])


def _mod_vec(mod_ref, is_ctx, batch_rows):
    return jnp.where(is_ctx, mod_ref[batch_rows:batch_rows + 1, :], mod_ref[pl.ds(pl.program_id(0), 1), :])


def _local_mix(p_loc, is_ctx, cw_ref, gmn_ref, gmw_ref, gmb_ref):
    sb = p_loc.shape[0]
    t_idx = lax.broadcasted_iota(jnp.int32, (sb, 1), 0)
    pos = jnp.where(is_ctx, t_idx, t_idx & (GRID_W - 1))
    last_pos = jnp.where(is_ctx, sb - 1, GRID_W - 1)
    gate_b = p_loc[:, 0:CV_W]
    z = p_loc[:, CV_W:2 * CV_W] * p_loc[:, 2 * CV_W:3 * CV_W]
    z_prev = jnp.where(pos == 0, 0.0, pltpu.roll(z, 1, 0))
    z_next = jnp.where(pos == last_pos, 0.0, pltpu.roll(z, sb - 1, 0))
    conv = gate_b * (cw_ref[0:1, :] * z_prev + cw_ref[1:2, :] * z + cw_ref[2:3, :] * z_next)
    u = p_loc[:, 3 * CV_W:3 * CV_W + GM_W]
    v = p_loc[:, 3 * CV_W + GM_W:3 * CV_W + 2 * GM_W]
    vn = (v * _rms_scale(v) * gmn_ref[...]).astype(_BF16)
    group = lax.broadcasted_iota(jnp.int32, (CHUNK, GM_W), 1) // (GM_W // GM_GROUPS)
    zero = jnp.zeros((CHUNK, GM_W), _BF16)
    zs = []
    for n in range(sb // CHUNK):
        vc = vn[n * CHUNK:(n + 1) * CHUNK]
        v_groups = jnp.concatenate([jnp.where(group == g, vc, zero) for g in range(GM_GROUPS)], axis=0)
        zs.append(gmb_ref[...] + _dot(gmw_ref[...], v_groups))
    gm = u * jnp.concatenate(zs, axis=0)
    return jnp.concatenate([conv.astype(_BF16), gm.astype(_BF16)], axis=1)


def _in_proj_kernel(*refs, n_sub, seq, batch_rows):
    x_refs, ctx_ref = refs[:n_sub], refs[n_sub]
    (sh_ref, sc_ref, nw_ref, w_ref, wkg_ref, cw_ref, gmn_ref, gmw_ref, gmb_ref,
     pml_ref, mix_ref, kt_ref, gi_ref, gf_ref) = refs[n_sub + 1:]
    sb = SUB_ROWS
    cps = sb // CHUNK
    k_scale = ML_DH ** -0.5

    def norm_stage(s):
        is_ctx, x = _residual_sub(x_refs, ctx_ref, s, seq)
        xn = x * _rms_scale(x) * nw_ref[...]
        hx = xn * (1.0 + _mod_vec(sc_ref, is_ctx, batch_rows)) + _mod_vec(sh_ref, is_ctx, batch_rows)
        return is_ctx, hx.astype(_BF16)

    def proj_stage(s, hb):
        rs = slice(s * sb, (s + 1) * sb)
        p = _dot(hb, w_ref[...])
        pml_ref[rs, :] = p[:, :P_ML_W].astype(_BF16)
        kg = lax.dot_general(wkg_ref[...], hb, (((1,), (1,)), ((), ())), preferred_element_type=_F32)
        for n in range(cps):
            c = s * cps + n
            lanes = slice(n * CHUNK, (n + 1) * CHUNK)
            kt_ref[c] = (kg[:ML_W, lanes] * k_scale).astype(_BF16)
            gi_ref[c * N_DIRHEAD:(c + 1) * N_DIRHEAD, :] = kg[ML_W:ML_W + N_DIRHEAD, lanes]
            gf_ref[c * N_DIRHEAD:(c + 1) * N_DIRHEAD, :] = kg[ML_W + N_DIRHEAD:, lanes]
        return p[:, P_ML_W:]

    def local_stage(s, is_ctx, p_loc):
        mix_ref[s * sb:(s + 1) * sb, :] = _local_mix(p_loc, is_ctx, cw_ref, gmn_ref, gmw_ref, gmb_ref)

    is_ctx, hb, p_loc = {}, {}, {}
    for k, s in _skewed_order(n_sub, 3):
        if k == 0:
            is_ctx[s], hb[s] = norm_stage(s)
        elif k == 1:
            p_loc[s] = proj_stage(s, hb.pop(s))
        else:
            local_stage(s, is_ctx.pop(s), p_loc.pop(s))


def _in_proj(x_arr, ctx_arr, mods, layer, norm_w, w_main, w_kg, conv_w, gm_norm, gm_w, gm_b,
             *, seq, total, ctx_block):
    bsz = x_arr.shape[0]
    tm = TILE_ROWS
    n_sub = tm // SUB_ROWS
    cpt = tm // CHUNK
    n_chunks = total // CHUNK
    kern = functools.partial(_in_proj_kernel, n_sub=n_sub, seq=seq, batch_rows=bsz)
    row_map = lambda b, i: (b, i, 0)
    lay = lambda *shape: pl.BlockSpec((None,) + shape, lambda b, i: (layer,) + (0,) * len(shape))
    return pl.pallas_call(
        kern,
        grid=(bsz, total // tm),
        in_specs=_residual_specs(x_arr.shape[1], tm, ctx_block) + [
            pl.BlockSpec((None, MOD_ROWS, D_MODEL), lambda b, i: (layer, 0, 0)),
            pl.BlockSpec((None, MOD_ROWS, D_MODEL), lambda b, i: (layer, 0, 1)),
            lay(1, D_MODEL),
            lay(D_MODEL, P_ML_W + P_LOC_W),
            lay(KG_ROWS, D_MODEL),
            lay(SUBLANES, CV_W),
            lay(1, GM_W),
            lay(CHUNK, GM_GROUPS * CHUNK),
            lay(CHUNK, GM_W),
        ],
        out_specs=[
            pl.BlockSpec((None, tm, P_ML_W), row_map),
            pl.BlockSpec((None, tm, MIX_LOC_W), row_map),
            pl.BlockSpec((None, cpt, ML_W, CHUNK), lambda b, i: (b, i, 0, 0)),
            pl.BlockSpec((None, cpt * N_DIRHEAD, CHUNK), row_map),
            pl.BlockSpec((None, cpt * N_DIRHEAD, CHUNK), row_map),
        ],
        out_shape=[
            jax.ShapeDtypeStruct((bsz, total, P_ML_W), _BF16),
            jax.ShapeDtypeStruct((bsz, total, MIX_LOC_W), _BF16),
            jax.ShapeDtypeStruct((bsz, n_chunks, ML_W, CHUNK), _BF16),
            jax.ShapeDtypeStruct((bsz, n_chunks * N_DIRHEAD, CHUNK), _F32),
            jax.ShapeDtypeStruct((bsz, n_chunks * N_DIRHEAD, CHUNK), _F32),
        ],
        compiler_params=pltpu.CompilerParams(vmem_limit_bytes=_VMEM_LIMIT),
        name="in_proj",
    )(*([x_arr] * n_sub), ctx_arr, mods, mods, norm_w, w_main, w_kg, conv_w, gm_norm, gm_w, gm_b)


def _directional_scan(x, combine, fill):
    row = lax.broadcasted_iota(jnp.int32, x.shape, 0)
    lane = lax.broadcasted_iota(jnp.int32, x.shape, 1)
    fwd = (row & (N_DIRHEAD - 1)) < ML_HEADS
    n = x.shape[1]
    s = 1
    while s < n:
        prev = jnp.where(lane >= s, pltpu.roll(x, s, 1), fill)
        nxt = jnp.where(lane < n - s, pltpu.roll(x, n - s, 1), fill)
        x = combine(x, jnp.where(fwd, prev, nxt))
        s *= 2
    return x


def _mlstm_kernel(p_ref, kt_ref, gi_ref, gf_ref, bi_ref, bf_ref, mln_ref, wo_ref, o_ref,
                  r_s, cm_s, b_s, bsum_s, rmax_s, cn_s, rhs_s, kti_s, hf_s, hb_s,
                  *, n_chunks, n_ctx_chunks, ctx_out):
    L = CHUNK
    R = N_DIRHEAD
    n_lat = n_chunks - n_ctx_chunks
    neg_inf = -jnp.inf

    tile_rows = lambda a: jnp.concatenate([a] * n_chunks, axis=0)
    li = gi_ref[...] + tile_rows(bi_ref[...])
    lf = jax.nn.log_sigmoid(gf_ref[...] + tile_rows(bf_ref[...]))
    b = _directional_scan(lf, jnp.add, 0.0)
    r = li - b
    r_s[...] = r
    b_s[...] = b
    cm_s[...] = _directional_scan(r, jnp.maximum, neg_inf)
    bsum_s[...] = jnp.broadcast_to(jnp.sum(lf, axis=1, keepdims=True), lf.shape)
    rmax_s[...] = jnp.broadcast_to(jnp.max(r, axis=1, keepdims=True), lf.shape)

    cn_s[...] = jnp.zeros_like(cn_s)
    eye = (lax.broadcasted_iota(jnp.int32, (ML_DH, ML_DH), 0)
           == lax.broadcasted_iota(jnp.int32, (ML_DH, ML_DH), 1)).astype(_BF16)
    for j in range(R):
        rhs_s[j, 0:L, ML_DH:] = jnp.ones((L, ML_DH), _BF16)
        rhs_s[j, L:, :] = jnp.zeros((ML_DH, 2 * ML_DH), _BF16)
        kti_s[j, :, L:] = eye

    def make_step(with_out):
        def step(i, m_prev):
            cf = jnp.where(i < n_ctx_chunks, n_lat + i, i - n_ctx_chunks)
            cb = n_chunks - 1 - i
            fwd_rows = lax.broadcasted_iota(jnp.int32, (R, L), 0) < ML_HEADS
            rows_f = pl.ds(pl.multiple_of(cf * R, R), R)
            rows_b = pl.ds(pl.multiple_of(cb * R, R), R)
            pick = lambda ref: jnp.where(fwd_rows, ref[rows_f, :], ref[rows_b, :])
            r8 = pick(r_s)
            rmax = pick(rmax_s)
            a8 = jnp.maximum(m_prev, pick(cm_s))
            e_negm = jnp.exp(-(pick(b_s) + a8))
            a_last = jnp.maximum(m_prev, rmax)
            decay = jnp.exp(m_prev - a_last)
            f_kw = jnp.exp(r8 - rmax) * jnp.exp(rmax - a_last)
            m_new = pick(bsum_s) + a_last
            pad_rows = jnp.zeros((LANES - 2 * R, L), _F32)
            cols = jnp.concatenate([a8 * LOG2_E, e_negm, pad_rows], axis=0).T
            r8_l2 = r8 * LOG2_E
            m_prev_l2 = m_prev * LOG2_E
            if with_out:
                t_idx = lax.broadcasted_iota(jnp.int32, (L, L), 0)
                s_idx = lax.broadcasted_iota(jnp.int32, (L, L), 1)
                causal = (s_idx <= t_idx, s_idx >= t_idx)

            chunk_of = lambda j: cf if j < ML_HEADS else cb
            rows_of = lambda j: pl.ds(pl.multiple_of(chunk_of(j) * L, L), L)
            head_of = lambda j: slice((j % ML_HEADS) * ML_DH, (j % ML_HEADS + 1) * ML_DH)
            kts, qk_qs, s_exts, zs, upds = [], [], [], [], []
            for j in range(R):
                kt = kt_ref[chunk_of(j), head_of(j), :]
                kts.append(kt)
                rhs_s[j, 0:L, 0:ML_DH] = p_ref[rows_of(j), head_of(j)]
                if with_out:
                    kti_s[j, :, 0:L] = kt
                    h = j % ML_HEADS
                    q = p_ref[rows_of(j), ML_W + h * ML_DH:ML_W + (h + 1) * ML_DH]
                    qk_qs.append(_dot(q, kti_s[j]))
            if with_out:
                for j in range(R):
                    a_col = jnp.broadcast_to(cols[:, j:j + 1], (L, L))
                    logw = jnp.concatenate(
                        [jnp.where(causal[j // ML_HEADS], r8_l2[j:j + 1, :] - a_col, neg_inf),
                         m_prev_l2[j:j + 1, :] - a_col], axis=1)
                    s_exts.append((qk_qs[j] * jnp.exp2(logw)).astype(_BF16))
            for j in range(R):
                if with_out:
                    zs.append(_dot(s_exts[j], rhs_s[j]))
                kw = (kts[j].astype(_F32) * f_kw[j:j + 1, :]).astype(_BF16)
                upds.append(_dot(kw, rhs_s[j, 0:L, :]))
            if with_out:
                for j in range(R):
                    z = zs[j]
                    hd_s = hf_s if j < ML_HEADS else hb_s
                    floor = jnp.broadcast_to(cols[:, R + j:R + j + 1], (L, ML_DH))
                    hd_s[rows_of(j), head_of(j)] = z[:, :ML_DH] / jnp.maximum(jnp.abs(z[:, ML_DH:]), floor)
            for j in range(R):
                cn = decay[j:j + 1, 0:1] * cn_s[j] + upds[j]
                cn_s[j] = cn
                rhs_s[j, L:, :] = cn.astype(_BF16)
            return m_new
        return step

    m0 = jnp.zeros((R, L), _F32)
    m1 = lax.fori_loop(0, n_ctx_chunks, make_step(ctx_out), m0)
    lax.fori_loop(n_ctx_chunks, n_chunks, make_step(True), m1)

    fin = FINISH_CHUNKS * L

    def gate_stage(k):
        rows = slice(k * fin, (k + 1) * fin)
        heads = []
        for h in range(ML_HEADS):
            hs = slice(h * ML_DH, (h + 1) * ML_DH)
            hh = hf_s[rows, hs] + hb_s[rows, hs]
            hn = hh * _rms_scale(hh) * mln_ref[:, hs]
            og = p_ref[rows, 2 * ML_W + h * ML_DH:2 * ML_W + (h + 1) * ML_DH].astype(_F32)
            o = 0.5 * jnp.tanh(0.5 * og) + 0.5
            heads.append((o * hn).astype(_BF16))
        return jnp.concatenate(heads, axis=1)

    def proj_stage(k, gated):
        o_ref[k * fin:(k + 1) * fin, :] = _dot(gated, wo_ref[...]).astype(_BF16)

    assert n_chunks % FINISH_CHUNKS == 0 and n_lat % FINISH_CHUNKS == 0
    gated = {}
    for stage, k in _skewed_order((n_chunks if ctx_out else n_lat) // FINISH_CHUNKS, 2):
        if stage == 0:
            gated[k] = gate_stage(k)
        else:
            proj_stage(k, gated.pop(k))
    if not ctx_out:
        o_ref[n_lat * L:, :] = jnp.zeros((n_ctx_chunks * L, D_MODEL), _BF16)


def _mlstm(p_ml, k_t, g_in, g_fg, bias_in, bias_fg, ml_norm, w_out_ml, *, n_ctx_chunks, ctx_out):
    bsz, t, _ = p_ml.shape
    n_chunks = t // CHUNK
    kern = functools.partial(_mlstm_kernel, n_chunks=n_chunks, n_ctx_chunks=n_ctx_chunks, ctx_out=ctx_out)
    gate_rows = n_chunks * N_DIRHEAD
    gate_scratch = pltpu.VMEM((gate_rows, CHUNK), _F32)
    return pl.pallas_call(
        kern,
        grid=(bsz,),
        in_specs=[
            pl.BlockSpec((None, t, P_ML_W), lambda b: (b, 0, 0)),
            pl.BlockSpec((None, n_chunks, ML_W, CHUNK), lambda b: (b, 0, 0, 0)),
            pl.BlockSpec((None, gate_rows, CHUNK), lambda b: (b, 0, 0)),
            pl.BlockSpec((None, gate_rows, CHUNK), lambda b: (b, 0, 0)),
            pl.BlockSpec((N_DIRHEAD, CHUNK), lambda b: (0, 0)),
            pl.BlockSpec((N_DIRHEAD, CHUNK), lambda b: (0, 0)),
            pl.BlockSpec((1, ML_W), lambda b: (0, 0)),
            pl.BlockSpec((ML_W, D_MODEL), lambda b: (0, 0)),
        ],
        out_specs=pl.BlockSpec((None, t, D_MODEL), lambda b: (b, 0, 0)),
        out_shape=jax.ShapeDtypeStruct((bsz, t, D_MODEL), _BF16),
        scratch_shapes=[
            gate_scratch,
            gate_scratch,
            gate_scratch,
            gate_scratch,
            gate_scratch,
            pltpu.VMEM((N_DIRHEAD, ML_DH, 2 * ML_DH), _F32),
            pltpu.VMEM((N_DIRHEAD, CHUNK + ML_DH, 2 * ML_DH), _BF16),
            pltpu.VMEM((N_DIRHEAD, ML_DH, CHUNK + ML_DH), _BF16),
            pltpu.VMEM((t, ML_W), _F32),
            pltpu.VMEM((t, ML_W), _F32),
        ],
        compiler_params=pltpu.CompilerParams(vmem_limit_bytes=_VMEM_LIMIT),
        name="mlstm",
    )(p_ml, k_t, g_in, g_fg, bias_in, bias_fg, ml_norm, w_out_ml)


def _out_ffn_kernel(*refs, n_sub, seq, batch_rows, final_norm):
    x_refs, ctx_ref = refs[:n_sub], refs[n_sub]
    (ml_ref, mix_ref, g1_ref, sh2_ref, sc2_ref, g2_ref, n2_ref, wo_ref, w1_ref, w3_ref, w2_ref, nf_ref,
     o_ref) = refs[n_sub + 1:]
    sb = SUB_ROWS

    def proj_stage(s):
        rs = slice(s * sb, (s + 1) * sb)
        is_ctx, x = _residual_sub(x_refs, ctx_ref, s, seq)
        proj = ml_ref[rs, :].astype(_F32) + _dot(mix_ref[rs, :], wo_ref[...])
        x1 = x + _mod_vec(g1_ref, is_ctx, batch_rows) * proj
        hx2 = x1 * _rms_scale(x1) * n2_ref[...]
        hx2 = hx2 * (1.0 + _mod_vec(sc2_ref, is_ctx, batch_rows)) + _mod_vec(sh2_ref, is_ctx, batch_rows)
        return is_ctx, x1, hx2.astype(_BF16)

    def up_stage(hx2):
        a = _dot(hx2, w1_ref[...])
        return (a * jax.nn.sigmoid(a) * _dot(hx2, w3_ref[...])).astype(_BF16)

    def down_stage(s, is_ctx, x1, act):
        x2 = x1 + _mod_vec(g2_ref, is_ctx, batch_rows) * _dot(act, w2_ref[...])
        if final_norm:
            x2 = x2 * _rms_scale(x2) * nf_ref[...]
        o_ref[s * sb:(s + 1) * sb, :] = x2

    is_ctx, x1, hx2, act = {}, {}, {}, {}
    for k, s in _skewed_order(n_sub, 3):
        if k == 0:
            is_ctx[s], x1[s], hx2[s] = proj_stage(s)
        elif k == 1:
            act[s] = up_stage(hx2.pop(s))
        else:
            down_stage(s, is_ctx.pop(s), x1.pop(s), act.pop(s))


def _out_ffn(x_arr, ctx_arr, ml, mix_loc, mods, layer, norm2, w_out, w1, w3, w2, norm_f,
             *, seq, ctx_block, out_rows, tile_rows, final_norm):
    bsz = x_arr.shape[0]
    tm = tile_rows
    n_sub = tm // SUB_ROWS
    kern = functools.partial(_out_ffn_kernel, n_sub=n_sub, seq=seq, batch_rows=bsz, final_norm=final_norm)
    row_map = lambda b, i: (b, i, 0)
    mod_spec = lambda k: pl.BlockSpec((None, MOD_ROWS, D_MODEL), lambda b, i: (layer, 0, k))
    lay = lambda *shape: pl.BlockSpec((None,) + shape, lambda b, i: (layer,) + (0,) * len(shape))
    return pl.pallas_call(
        kern,
        grid=(bsz, out_rows // tm),
        in_specs=_residual_specs(x_arr.shape[1], tm, ctx_block) + [
            pl.BlockSpec((None, tm, D_MODEL), row_map),
            pl.BlockSpec((None, tm, MIX_LOC_W), row_map),
            mod_spec(2), mod_spec(3), mod_spec(4), mod_spec(5),
            lay(1, D_MODEL),
            lay(MIX_LOC_W, D_MODEL),
            lay(D_MODEL, D_FF),
            lay(D_MODEL, D_FF),
            lay(D_FF, D_MODEL),
            pl.BlockSpec((1, D_MODEL), lambda b, i: (0, 0)),
        ],
        out_specs=pl.BlockSpec((None, tm, D_MODEL), row_map),
        out_shape=jax.ShapeDtypeStruct((bsz, out_rows, D_MODEL), _F32),
        compiler_params=pltpu.CompilerParams(vmem_limit_bytes=_VMEM_LIMIT),
        name="out_ffn",
    )(*([x_arr] * n_sub), ctx_arr, ml, mix_loc, mods, mods, mods, mods, norm2, w_out, w1, w3, w2, norm_f)


def _prepare_in_weights(w_in, b_gates):
    w_bf = w_in.astype(_BF16)
    sl = lambda off, width: w_bf[:, :, off:off + width]
    w_main = jnp.concatenate(
        [sl(_OFF_V, ML_W), sl(_OFF_Q, ML_W), sl(_OFF_O, ML_W), sl(_OFF_CV, 3 * CV_W), sl(_OFF_GM, 2 * GM_W)],
        axis=-1)
    kinds = (0, 2, 1, 3)
    w_kg = jnp.concatenate([sl(_OFF_K, ML_W)] + [sl(_OFF_G + kind * ML_HEADS, ML_HEADS) for kind in kinds], axis=-1)
    w_kg = jnp.swapaxes(w_kg, 1, 2)
    bias = jnp.concatenate([b_gates[:, kind * ML_HEADS:(kind + 1) * ML_HEADS] for kind in kinds], axis=-1)
    bias = jnp.broadcast_to(bias.astype(_F32)[:, :, None], bias.shape + (CHUNK,))
    return w_main, w_kg, bias[:, :N_DIRHEAD], bias[:, N_DIRHEAD:]


def kernel(x, c, ctx, c_ctx, w_ada, b_ada, norm1, norm2, w_in, b_gates, ml_norm, conv_w, gm_norm, gm_ws,
           gm_bs, w_out, w1, w3, w2, norm_f):
    bsz, seq, _ = x.shape
    ctx_len = ctx.shape[1]
    depth = w_in.shape[0]
    total = seq + ctx_len
    assert bsz < MOD_ROWS and w_in.shape[-1] == _D_IN and w1.shape[-1] == D_FF
    assert seq % GRID_W == 0 and seq % CHUNK == 0 and ctx_len % CHUNK == 0
    assert total % TILE_ROWS == 0 and TILE_ROWS % CHUNK == 0 and total - TILE_ROWS <= seq
    assert seq % LAST_TILE_ROWS == 0 and TILE_ROWS % SUB_ROWS == 0 and LAST_TILE_ROWS % SUB_ROWS == 0
    assert seq % SUB_ROWS == 0 and ctx_len == SUB_ROWS and SUB_ROWS % CHUNK == 0 and SUB_ROWS % GRID_W == 0
    n_ctx_chunks = ctx_len // CHUNK

    cc = jnp.concatenate([c, c_ctx[None, :], jnp.zeros((MOD_ROWS - bsz - 1, D_MODEL), _F32)], axis=0)
    mods = _ada_table(cc, w_ada, b_ada)

    w_main, w_kg, bias_in, bias_fg = _prepare_in_weights(w_in, b_gates)
    wo_b, w1_b, w3_b, w2_b = (w.astype(_BF16) for w in (w_out, w1, w3, w2))
    gm_w_b = jnp.swapaxes(gm_ws, 1, 2).reshape(depth, CHUNK, GM_GROUPS * CHUNK).astype(_BF16)
    gm_bias = jnp.repeat(jnp.swapaxes(gm_bs, 1, 2), GM_W // GM_GROUPS, axis=2)
    conv_w8 = jnp.pad(conv_w, ((0, 0), (0, SUBLANES - conv_w.shape[1]), (0, 0)))
    norm1_r = norm1.reshape(depth, 1, D_MODEL)
    norm2_r = norm2.reshape(depth, 1, D_MODEL)
    gm_norm_r = gm_norm.reshape(depth, 1, GM_W)
    ml_norm_r = ml_norm.reshape(depth, 1, ML_W)
    norm_f_r = norm_f.reshape(1, D_MODEL)

    x_arr, ctx_arr, ctx_block = x, ctx, 0
    for l in range(depth):
        last = l == depth - 1
        p_ml, mix_loc, k_t, g_in, g_fg = _in_proj(x_arr, ctx_arr, mods, l, norm1_r, w_main, w_kg, conv_w8,
                                                  gm_norm_r, gm_w_b, gm_bias,
                                                  seq=seq, total=total, ctx_block=ctx_block)
        ml = _mlstm(p_ml, k_t, g_in, g_fg, bias_in[l], bias_fg[l], ml_norm_r[l], wo_b[l, :ML_W],
                    n_ctx_chunks=n_ctx_chunks, ctx_out=not last)
        xs = _out_ffn(x_arr, ctx_arr, ml, mix_loc, mods, l, norm2_r, wo_b[:, ML_W:], w1_b, w3_b, w2_b, norm_f_r,
                      seq=seq, ctx_block=ctx_block, out_rows=seq if last else total,
                      tile_rows=LAST_TILE_ROWS if last else TILE_ROWS, final_norm=last)
        x_arr, ctx_arr, ctx_block = xs, xs, seq // SUB_ROWS
    return xs
```

```python
import functools

import jax
import jax.numpy as jnp
from jax import lax
from jax.experimental import pallas as pl
from jax.experimental.pallas import tpu as pltpu

D_MODEL = 1024
GRID_W = 64
ML_W = D_MODEL // 2
ML_HEADS = 4
ML_DH = ML_W // ML_HEADS
CHUNK = 128
CV_W = D_MODEL // 4
GM_W = D_MODEL // 4
GM_GROUPS = 4
N_GATES = 4 * ML_HEADS
D_FF = 2816
EPS = 1e-6
LOG2_E = 1.4426950408889634
LANES = 128
SUBLANES = 8
N_DIRHEAD = 2 * ML_HEADS
FINISH_CHUNKS = 2

_OFF_K = 0
_OFF_V = _OFF_K + ML_W
_OFF_G = _OFF_V + ML_W
_OFF_Q = _OFF_G + N_GATES
_OFF_O = _OFF_Q + ML_W
_OFF_CV = _OFF_O + ML_W
_OFF_GM = _OFF_CV + 3 * CV_W
_D_IN = _OFF_GM + 2 * GM_W

P_ML_W = 3 * ML_W
P_LOC_W = 3 * CV_W + 2 * GM_W
MIX_LOC_W = CV_W + GM_W
KG_ROWS = ML_W + N_GATES
KG_COLS = ML_W + LANES
MOD_ROWS = 24
TILE_ROWS = 768
LAST_TILE_ROWS = 1024
SUB_ROWS = 256

_VMEM_LIMIT = 56 * 1024 * 1024

_BF16 = jnp.bfloat16
_F32 = jnp.float32


def _dot(a, b):
    return jnp.dot(a, b, preferred_element_type=_F32)


def _rms_scale(x):
    return lax.rsqrt(jnp.mean(x * x, axis=-1, keepdims=True) + EPS)


def _ada_kernel(c_ref, w_ref, b_ref, o_ref):
    c = c_ref[...]
    s = (c * jax.nn.sigmoid(c)).astype(_BF16)
    o_ref[...] = _dot(s, w_ref[...].astype(_BF16)) + b_ref[...]


def _ada_table(cc, w_ada, b_ada):
    depth, _, n = w_ada.shape
    tn = 1536
    return pl.pallas_call(
        _ada_kernel,
        grid=(depth, n // tn),
        in_specs=[
            pl.BlockSpec((MOD_ROWS, D_MODEL), lambda l, j: (0, 0)),
            pl.BlockSpec((None, D_MODEL, tn), lambda l, j: (l, 0, j)),
            pl.BlockSpec((None, 1, tn), lambda l, j: (l, 0, j)),
        ],
        out_specs=pl.BlockSpec((None, MOD_ROWS, tn), lambda l, j: (l, 0, j)),
        out_shape=jax.ShapeDtypeStruct((depth, MOD_ROWS, n), _F32),
        compiler_params=pltpu.CompilerParams(vmem_limit_bytes=_VMEM_LIMIT),
        name="ada_table",
    )(cc, w_ada, b_ada.reshape(depth, 1, n))


def _skewed_order(n_sub, n_stages):
    return [(k, t - k) for t in range(n_sub + n_stages - 1) for k in range(n_stages) if 0 <= t - k < n_sub]


def _residual_specs(x_rows, tm, ctx_block):
    n_sub = tm // SUB_ROWS
    n_blocks = x_rows // SUB_ROWS

    def sub_map(b, i, *, s):
        return (b, jnp.minimum(i * n_sub + s, n_blocks - 1), 0)

    subs = [pl.BlockSpec((None, SUB_ROWS, D_MODEL), functools.partial(sub_map, s=s)) for s in range(n_sub)]
    return subs + [pl.BlockSpec((None, SUB_ROWS, D_MODEL), lambda b, i: (b, ctx_block, 0))]


def _residual_sub(x_refs, ctx_ref, s, seq):
    n_sub = len(x_refs)
    is_ctx = (pl.program_id(1) * n_sub + s) * SUB_ROWS >= seq
    return is_ctx, jnp.where(is_ctx, ctx_ref[...], x_refs[s][...])


def _mod_vec(mod_ref, is_ctx, batch_rows):
    return jnp.where(is_ctx, mod_ref[batch_rows:batch_rows + 1, :], mod_ref[pl.ds(pl.program_id(0), 1), :])


def _local_mix(p_loc, is_ctx, cw_ref, gmn_ref, gmw_ref, gmb_ref):
    sb = p_loc.shape[0]
    t_idx = lax.broadcasted_iota(jnp.int32, (sb, 1), 0)
    pos = jnp.where(is_ctx, t_idx, t_idx & (GRID_W - 1))
    last_pos = jnp.where(is_ctx, sb - 1, GRID_W - 1)
    gate_b = p_loc[:, 0:CV_W]
    z = p_loc[:, CV_W:2 * CV_W] * p_loc[:, 2 * CV_W:3 * CV_W]
    z_prev = jnp.where(pos == 0, 0.0, pltpu.roll(z, 1, 0))
    z_next = jnp.where(pos == last_pos, 0.0, pltpu.roll(z, sb - 1, 0))
    conv = gate_b * (cw_ref[0:1, :] * z_prev + cw_ref[1:2, :] * z + cw_ref[2:3, :] * z_next)
    u = p_loc[:, 3 * CV_W:3 * CV_W + GM_W]
    v = p_loc[:, 3 * CV_W + GM_W:3 * CV_W + 2 * GM_W]
    vn = (v * _rms_scale(v) * gmn_ref[...]).astype(_BF16)
    group = lax.broadcasted_iota(jnp.int32, (CHUNK, GM_W), 1) // (GM_W // GM_GROUPS)
    zero = jnp.zeros((CHUNK, GM_W), _BF16)
    zs = []
    for n in range(sb // CHUNK):
        vc = vn[n * CHUNK:(n + 1) * CHUNK]
        v_groups = jnp.concatenate([jnp.where(group == g, vc, zero) for g in range(GM_GROUPS)], axis=0)
        zs.append(gmb_ref[...] + _dot(gmw_ref[...], v_groups))
    gm = u * jnp.concatenate(zs, axis=0)
    return jnp.concatenate([conv.astype(_BF16), gm.astype(_BF16)], axis=1)


def _in_proj_kernel(*refs, n_sub, seq, batch_rows):
    x_refs, ctx_ref = refs[:n_sub], refs[n_sub]
    (sh_ref, sc_ref, nw_ref, w_ref, wkg_ref, cw_ref, gmn_ref, gmw_ref, gmb_ref,
     pml_ref, mix_ref, kt_ref, gi_ref, gf_ref, wkg_t) = refs[n_sub + 1:]
    sb = SUB_ROWS

    @pl.when((pl.program_id(0) == 0) & (pl.program_id(1) == 0))
    def _():
        wkg_t[...] = wkg_ref[...].astype(_F32).T.astype(_BF16)

    cps = sb // CHUNK
    k_scale = ML_DH ** -0.5

    def norm_stage(s):
        is_ctx, x = _residual_sub(x_refs, ctx_ref, s, seq)
        xn = x * _rms_scale(x) * nw_ref[...]
        hx = xn * (1.0 + _mod_vec(sc_ref, is_ctx, batch_rows)) + _mod_vec(sh_ref, is_ctx, batch_rows)
        return is_ctx, hx.astype(_BF16)

    def proj_stage(s, hb):
        rs = slice(s * sb, (s + 1) * sb)
        p = _dot(hb, w_ref[...])
        pml_ref[rs, :] = p[:, :P_ML_W].astype(_BF16)
        kg = lax.dot_general(wkg_t[0:KG_ROWS, :], hb, (((1,), (1,)), ((), ())), preferred_element_type=_F32)
        for n in range(cps):
            c = s * cps + n
            lanes = slice(n * CHUNK, (n + 1) * CHUNK)
            kt_ref[c] = (kg[:ML_W, lanes] * k_scale).astype(_BF16)
            gi_ref[c * N_DIRHEAD:(c + 1) * N_DIRHEAD, :] = kg[ML_W:ML_W + N_DIRHEAD, lanes]
            gf_ref[c * N_DIRHEAD:(c + 1) * N_DIRHEAD, :] = kg[ML_W + N_DIRHEAD:, lanes]
        return p[:, P_ML_W:]

    def local_stage(s, is_ctx, p_loc):
        mix_ref[s * sb:(s + 1) * sb, :] = _local_mix(p_loc, is_ctx, cw_ref, gmn_ref, gmw_ref, gmb_ref)

    is_ctx, hb, p_loc = {}, {}, {}
    for k, s in _skewed_order(n_sub, 3):
        if k == 0:
            is_ctx[s], hb[s] = norm_stage(s)
        elif k == 1:
            p_loc[s] = proj_stage(s, hb.pop(s))
        else:
            local_stage(s, is_ctx.pop(s), p_loc.pop(s))


def _in_proj(x_arr, ctx_arr, mods, layer, norm_w, w_main, w_kg, conv_w, gm_norm, gm_w, gm_b,
             *, seq, total, ctx_block):
    bsz = x_arr.shape[0]
    tm = TILE_ROWS
    n_sub = tm // SUB_ROWS
    cpt = tm // CHUNK
    n_chunks = total // CHUNK
    kern = functools.partial(_in_proj_kernel, n_sub=n_sub, seq=seq, batch_rows=bsz)
    row_map = lambda b, i: (b, i, 0)
    lay = lambda *shape: pl.BlockSpec((None,) + shape, lambda b, i: (layer,) + (0,) * len(shape))
    return pl.pallas_call(
        kern,
        grid=(bsz, total // tm),
        in_specs=_residual_specs(x_arr.shape[1], tm, ctx_block) + [
            pl.BlockSpec((None, MOD_ROWS, D_MODEL), lambda b, i: (layer, 0, 0)),
            pl.BlockSpec((None, MOD_ROWS, D_MODEL), lambda b, i: (layer, 0, 1)),
            lay(1, D_MODEL),
            lay(D_MODEL, P_ML_W + P_LOC_W),
            lay(D_MODEL, KG_COLS),
            lay(SUBLANES, CV_W),
            lay(1, GM_W),
            lay(CHUNK, GM_GROUPS * CHUNK),
            lay(CHUNK, GM_W),
        ],
        out_specs=[
            pl.BlockSpec((None, tm, P_ML_W), row_map),
            pl.BlockSpec((None, tm, MIX_LOC_W), row_map),
            pl.BlockSpec((None, cpt, ML_W, CHUNK), lambda b, i: (b, i, 0, 0)),
            pl.BlockSpec((None, cpt * N_DIRHEAD, CHUNK), row_map),
            pl.BlockSpec((None, cpt * N_DIRHEAD, CHUNK), row_map),
        ],
        out_shape=[
            jax.ShapeDtypeStruct((bsz, total, P_ML_W), _BF16),
            jax.ShapeDtypeStruct((bsz, total, MIX_LOC_W), _BF16),
            jax.ShapeDtypeStruct((bsz, n_chunks, ML_W, CHUNK), _BF16),
            jax.ShapeDtypeStruct((bsz, n_chunks * N_DIRHEAD, CHUNK), _F32),
            jax.ShapeDtypeStruct((bsz, n_chunks * N_DIRHEAD, CHUNK), _F32),
        ],
        scratch_shapes=[pltpu.VMEM((KG_COLS, D_MODEL), _BF16)],
        compiler_params=pltpu.CompilerParams(vmem_limit_bytes=_VMEM_LIMIT),
        name="in_proj",
    )(*([x_arr] * n_sub), ctx_arr, mods, mods, norm_w, w_main, w_kg, conv_w, gm_norm, gm_w, gm_b)


def _directional_scan(x, combine, fill):
    row = lax.broadcasted_iota(jnp.int32, x.shape, 0)
    lane = lax.broadcasted_iota(jnp.int32, x.shape, 1)
    fwd = (row & (N_DIRHEAD - 1)) < ML_HEADS
    n = x.shape[1]
    s = 1
    while s < n:
        prev = jnp.where(lane >= s, pltpu.roll(x, s, 1), fill)
        nxt = jnp.where(lane < n - s, pltpu.roll(x, n - s, 1), fill)
        x = combine(x, jnp.where(fwd, prev, nxt))
        s *= 2
    return x


def _mlstm_kernel(p_ref, kt_ref, gi_ref, gf_ref, bi_ref, bf_ref, mln_ref, wo_ref, o_ref,
                  r_s, cm_s, b_s, bsum_s, rmax_s, cn_s, rhs_s, kti_s, hf_s, hb_s,
                  *, n_chunks, n_ctx_chunks, ctx_out):
    L = CHUNK
    R = N_DIRHEAD
    n_lat = n_chunks - n_ctx_chunks
    neg_inf = -jnp.inf

    tile_rows = lambda a: jnp.concatenate([a] * n_chunks, axis=0)
    li = gi_ref[...] + tile_rows(bi_ref[...])
    lf = jax.nn.log_sigmoid(gf_ref[...] + tile_rows(bf_ref[...]))
    b = _directional_scan(lf, jnp.add, 0.0)
    r = li - b
    r_s[...] = r
    b_s[...] = b
    cm_s[...] = _directional_scan(r, jnp.maximum, neg_inf)
    bsum_s[...] = jnp.broadcast_to(jnp.sum(lf, axis=1, keepdims=True), lf.shape)
    rmax_s[...] = jnp.broadcast_to(jnp.max(r, axis=1, keepdims=True), lf.shape)

    cn_s[...] = jnp.zeros_like(cn_s)
    eye = (lax.broadcasted_iota(jnp.int32, (ML_DH, ML_DH), 0)
           == lax.broadcasted_iota(jnp.int32, (ML_DH, ML_DH), 1)).astype(_BF16)
    for j in range(R):
        rhs_s[j, 0:L, ML_DH:] = jnp.ones((L, ML_DH), _BF16)
        rhs_s[j, L:, :] = jnp.zeros((ML_DH, 2 * ML_DH), _BF16)
        kti_s[j, :, L:] = eye

    def make_step(with_out):
        def step(i, m_prev):
            cf = jnp.where(i < n_ctx_chunks, n_lat + i, i - n_ctx_chunks)
            cb = n_chunks - 1 - i
            fwd_rows = lax.broadcasted_iota(jnp.int32, (R, L), 0) < ML_HEADS
            rows_f = pl.ds(pl.multiple_of(cf * R, R), R)
            rows_b = pl.ds(pl.multiple_of(cb * R, R), R)
            pick = lambda ref: jnp.where(fwd_rows, ref[rows_f, :], ref[rows_b, :])
            r8 = pick(r_s)
            rmax = pick(rmax_s)
            a8 = jnp.maximum(m_prev, pick(cm_s))
            e_negm = jnp.exp(-(pick(b_s) + a8))
            a_last = jnp.maximum(m_prev, rmax)
            decay = jnp.exp(m_prev - a_last)
            f_kw = jnp.exp(r8 - rmax) * jnp.exp(rmax - a_last)
            m_new = pick(bsum_s) + a_last
            pad_rows = jnp.zeros((LANES - 2 * R, L), _F32)
            cols = jnp.concatenate([a8 * LOG2_E, e_negm, pad_rows], axis=0).T
            r8_l2 = r8 * LOG2_E
            m_prev_l2 = m_prev * LOG2_E
            if with_out:
                t_idx = lax.broadcasted_iota(jnp.int32, (L, L), 0)
                s_idx = lax.broadcasted_iota(jnp.int32, (L, L), 1)
                causal = (s_idx <= t_idx, s_idx >= t_idx)

            chunk_of = lambda j: cf if j < ML_HEADS else cb
            rows_of = lambda j: pl.ds(pl.multiple_of(chunk_of(j) * L, L), L)
            head_of = lambda j: slice((j % ML_HEADS) * ML_DH, (j % ML_HEADS + 1) * ML_DH)
            kts, qk_qs, s_exts, zs, upds = [], [], [], [], []
            for j in range(R):
                kt = kt_ref[chunk_of(j), head_of(j), :]
                kts.append(kt)
                rhs_s[j, 0:L, 0:ML_DH] = p_ref[rows_of(j), head_of(j)]
                if with_out:
                    kti_s[j, :, 0:L] = kt
                    h = j % ML_HEADS
                    q = p_ref[rows_of(j), ML_W + h * ML_DH:ML_W + (h + 1) * ML_DH]
                    qk_qs.append(_dot(q, kti_s[j]))
            if with_out:
                for j in range(R):
                    a_col = jnp.broadcast_to(cols[:, j:j + 1], (L, L))
                    logw = jnp.concatenate(
                        [jnp.where(causal[j // ML_HEADS], r8_l2[j:j + 1, :] - a_col, neg_inf),
                         m_prev_l2[j:j + 1, :] - a_col], axis=1)
                    s_exts.append((qk_qs[j] * jnp.exp2(logw)).astype(_BF16))
            for j in range(R):
                if with_out:
                    zs.append(_dot(s_exts[j], rhs_s[j]))
                kw = (kts[j].astype(_F32) * f_kw[j:j + 1, :]).astype(_BF16)
                upds.append(_dot(kw, rhs_s[j, 0:L, :]))
            if with_out:
                for j in range(R):
                    z = zs[j]
                    hd_s = hf_s if j < ML_HEADS else hb_s
                    floor = jnp.broadcast_to(cols[:, R + j:R + j + 1], (L, ML_DH))
                    hd_s[rows_of(j), head_of(j)] = z[:, :ML_DH] / jnp.maximum(jnp.abs(z[:, ML_DH:]), floor)
            for j in range(R):
                cn = decay[j:j + 1, 0:1] * cn_s[j] + upds[j]
                cn_s[j] = cn
                rhs_s[j, L:, :] = cn.astype(_BF16)
            return m_new
        return step

    m0 = jnp.zeros((R, L), _F32)
    m1 = lax.fori_loop(0, n_ctx_chunks, make_step(ctx_out), m0)
    lax.fori_loop(n_ctx_chunks, n_chunks, make_step(True), m1)

    fin = FINISH_CHUNKS * L

    def gate_stage(k):
        rows = slice(k * fin, (k + 1) * fin)
        heads = []
        for h in range(ML_HEADS):
            hs = slice(h * ML_DH, (h + 1) * ML_DH)
            hh = hf_s[rows, hs] + hb_s[rows, hs]
            hn = hh * _rms_scale(hh) * mln_ref[:, hs]
            og = p_ref[rows, 2 * ML_W + h * ML_DH:2 * ML_W + (h + 1) * ML_DH].astype(_F32)
            o = 0.5 * jnp.tanh(0.5 * og) + 0.5
            heads.append((o * hn).astype(_BF16))
        return jnp.concatenate(heads, axis=1)

    def proj_stage(k, gated):
        o_ref[k * fin:(k + 1) * fin, :] = _dot(gated, wo_ref[...]).astype(_BF16)

    assert n_chunks % FINISH_CHUNKS == 0 and n_lat % FINISH_CHUNKS == 0
    gated = {}
    for stage, k in _skewed_order((n_chunks if ctx_out else n_lat) // FINISH_CHUNKS, 2):
        if stage == 0:
            gated[k] = gate_stage(k)
        else:
            proj_stage(k, gated.pop(k))
    if not ctx_out:
        o_ref[n_lat * L:, :] = jnp.zeros((n_ctx_chunks * L, D_MODEL), _BF16)


def _mlstm(p_ml, k_t, g_in, g_fg, bias_in, bias_fg, ml_norm, w_out_ml, *, n_ctx_chunks, ctx_out):
    bsz, t, _ = p_ml.shape
    n_chunks = t // CHUNK
    kern = functools.partial(_mlstm_kernel, n_chunks=n_chunks, n_ctx_chunks=n_ctx_chunks, ctx_out=ctx_out)
    gate_rows = n_chunks * N_DIRHEAD
    gate_scratch = pltpu.VMEM((gate_rows, CHUNK), _F32)
    return pl.pallas_call(
        kern,
        grid=(bsz,),
        in_specs=[
            pl.BlockSpec((None, t, P_ML_W), lambda b: (b, 0, 0)),
            pl.BlockSpec((None, n_chunks, ML_W, CHUNK), lambda b: (b, 0, 0, 0)),
            pl.BlockSpec((None, gate_rows, CHUNK), lambda b: (b, 0, 0)),
            pl.BlockSpec((None, gate_rows, CHUNK), lambda b: (b, 0, 0)),
            pl.BlockSpec((N_DIRHEAD, CHUNK), lambda b: (0, 0)),
            pl.BlockSpec((N_DIRHEAD, CHUNK), lambda b: (0, 0)),
            pl.BlockSpec((1, ML_W), lambda b: (0, 0)),
            pl.BlockSpec((ML_W, D_MODEL), lambda b: (0, 0)),
        ],
        out_specs=pl.BlockSpec((None, t, D_MODEL), lambda b: (b, 0, 0)),
        out_shape=jax.ShapeDtypeStruct((bsz, t, D_MODEL), _BF16),
        scratch_shapes=[
            gate_scratch,
            gate_scratch,
            gate_scratch,
            gate_scratch,
            gate_scratch,
            pltpu.VMEM((N_DIRHEAD, ML_DH, 2 * ML_DH), _F32),
            pltpu.VMEM((N_DIRHEAD, CHUNK + ML_DH, 2 * ML_DH), _BF16),
            pltpu.VMEM((N_DIRHEAD, ML_DH, CHUNK + ML_DH), _BF16),
            pltpu.VMEM((t, ML_W), _F32),
            pltpu.VMEM((t, ML_W), _F32),
        ],
        compiler_params=pltpu.CompilerParams(vmem_limit_bytes=_VMEM_LIMIT),
        name="mlstm",
    )(p_ml, k_t, g_in, g_fg, bias_in, bias_fg, ml_norm, w_out_ml)


def _out_ffn_kernel(*refs, n_sub, seq, batch_rows, final_norm):
    x_refs, ctx_ref = refs[:n_sub], refs[n_sub]
    (ml_ref, mix_ref, g1_ref, sh2_ref, sc2_ref, g2_ref, n2_ref, wo_ref, w1_ref, w3_ref, w2_ref, nf_ref,
     o_ref) = refs[n_sub + 1:]
    sb = SUB_ROWS

    def proj_stage(s):
        rs = slice(s * sb, (s + 1) * sb)
        is_ctx, x = _residual_sub(x_refs, ctx_ref, s, seq)
        proj = ml_ref[rs, :].astype(_F32) + _dot(mix_ref[rs, :], wo_ref[...])
        x1 = x + _mod_vec(g1_ref, is_ctx, batch_rows) * proj
        hx2 = x1 * _rms_scale(x1) * n2_ref[...]
        hx2 = hx2 * (1.0 + _mod_vec(sc2_ref, is_ctx, batch_rows)) + _mod_vec(sh2_ref, is_ctx, batch_rows)
        return is_ctx, x1, hx2.astype(_BF16)

    def up_stage(hx2):
        a = _dot(hx2, w1_ref[...])
        return (a * jax.nn.sigmoid(a) * _dot(hx2, w3_ref[...])).astype(_BF16)

    def down_stage(s, is_ctx, x1, act):
        x2 = x1 + _mod_vec(g2_ref, is_ctx, batch_rows) * _dot(act, w2_ref[...])
        if final_norm:
            x2 = x2 * _rms_scale(x2) * nf_ref[...]
        o_ref[s * sb:(s + 1) * sb, :] = x2

    is_ctx, x1, hx2, act = {}, {}, {}, {}
    for k, s in _skewed_order(n_sub, 3):
        if k == 0:
            is_ctx[s], x1[s], hx2[s] = proj_stage(s)
        elif k == 1:
            act[s] = up_stage(hx2.pop(s))
        else:
            down_stage(s, is_ctx.pop(s), x1.pop(s), act.pop(s))


def _out_ffn(x_arr, ctx_arr, ml, mix_loc, mods, layer, norm2, w_out, w1, w3, w2, norm_f,
             *, seq, ctx_block, out_rows, tile_rows, final_norm):
    bsz = x_arr.shape[0]
    tm = tile_rows
    n_sub = tm // SUB_ROWS
    kern = functools.partial(_out_ffn_kernel, n_sub=n_sub, seq=seq, batch_rows=bsz, final_norm=final_norm)
    row_map = lambda b, i: (b, i, 0)
    mod_spec = lambda k: pl.BlockSpec((None, MOD_ROWS, D_MODEL), lambda b, i: (layer, 0, k))
    lay = lambda *shape: pl.BlockSpec((None,) + shape, lambda b, i: (layer,) + (0,) * len(shape))
    return pl.pallas_call(
        kern,
        grid=(bsz, out_rows // tm),
        in_specs=_residual_specs(x_arr.shape[1], tm, ctx_block) + [
            pl.BlockSpec((None, tm, D_MODEL), row_map),
            pl.BlockSpec((None, tm, MIX_LOC_W), row_map),
            mod_spec(2), mod_spec(3), mod_spec(4), mod_spec(5),
            lay(1, D_MODEL),
            lay(MIX_LOC_W, D_MODEL),
            lay(D_MODEL, D_FF),
            lay(D_MODEL, D_FF),
            lay(D_FF, D_MODEL),
            pl.BlockSpec((1, D_MODEL), lambda b, i: (0, 0)),
        ],
        out_specs=pl.BlockSpec((None, tm, D_MODEL), row_map),
        out_shape=jax.ShapeDtypeStruct((bsz, out_rows, D_MODEL), _F32),
        compiler_params=pltpu.CompilerParams(vmem_limit_bytes=_VMEM_LIMIT),
        name="out_ffn",
    )(*([x_arr] * n_sub), ctx_arr, ml, mix_loc, mods, mods, mods, mods, norm2, w_out, w1, w3, w2, norm_f)


def _prepare_in_weights(w_in, b_gates):
    sl = lambda off, width: w_in[:, :, off:off + width]
    w_main = jnp.concatenate(
        [sl(_OFF_V, ML_W), sl(_OFF_Q, ML_W), sl(_OFF_O, ML_W), sl(_OFF_CV, 3 * CV_W), sl(_OFF_GM, 2 * GM_W)],
        axis=-1).astype(_BF16)
    kinds = (0, 2, 1, 3)
    pad = jnp.zeros(w_in.shape[:2] + (KG_COLS - KG_ROWS,), w_in.dtype)
    w_kg = jnp.concatenate(
        [sl(_OFF_K, ML_W)] + [sl(_OFF_G + kind * ML_HEADS, ML_HEADS) for kind in kinds] + [pad], axis=-1).astype(_BF16)
    bias = jnp.concatenate([b_gates[:, kind * ML_HEADS:(kind + 1) * ML_HEADS] for kind in kinds], axis=-1)
    bias = jnp.broadcast_to(bias.astype(_F32)[:, :, None], bias.shape + (CHUNK,))
    return w_main, w_kg, bias[:, :N_DIRHEAD], bias[:, N_DIRHEAD:]


def kernel(x, c, ctx, c_ctx, w_ada, b_ada, norm1, norm2, w_in, b_gates, ml_norm, conv_w, gm_norm, gm_ws,
           gm_bs, w_out, w1, w3, w2, norm_f):
    bsz, seq, _ = x.shape
    ctx_len = ctx.shape[1]
    depth = w_in.shape[0]
    total = seq + ctx_len
    assert bsz < MOD_ROWS and w_in.shape[-1] == _D_IN and w1.shape[-1] == D_FF
    assert seq % GRID_W == 0 and seq % CHUNK == 0 and ctx_len % CHUNK == 0
    assert total % TILE_ROWS == 0 and TILE_ROWS % CHUNK == 0 and total - TILE_ROWS <= seq
    assert seq % LAST_TILE_ROWS == 0 and TILE_ROWS % SUB_ROWS == 0 and LAST_TILE_ROWS % SUB_ROWS == 0
    assert seq % SUB_ROWS == 0 and ctx_len == SUB_ROWS and SUB_ROWS % CHUNK == 0 and SUB_ROWS % GRID_W == 0
    n_ctx_chunks = ctx_len // CHUNK

    cc = jnp.concatenate([c, c_ctx[None, :], jnp.zeros((MOD_ROWS - bsz - 1, D_MODEL), _F32)], axis=0)
    mods = _ada_table(cc, w_ada, b_ada)

    w_main, w_kg, bias_in, bias_fg = _prepare_in_weights(w_in, b_gates)
    wo_b, w1_b, w3_b, w2_b = (w.astype(_BF16) for w in (w_out, w1, w3, w2))
    gm_w_b = jnp.swapaxes(gm_ws, 1, 2).reshape(depth, CHUNK, GM_GROUPS * CHUNK).astype(_BF16)
    gm_bias = jnp.repeat(jnp.swapaxes(gm_bs, 1, 2), GM_W // GM_GROUPS, axis=2)
    conv_w8 = jnp.pad(conv_w, ((0, 0), (0, SUBLANES - conv_w.shape[1]), (0, 0)))
    norm1_r = norm1.reshape(depth, 1, D_MODEL)
    norm2_r = norm2.reshape(depth, 1, D_MODEL)
    gm_norm_r = gm_norm.reshape(depth, 1, GM_W)
    ml_norm_r = ml_norm.reshape(depth, 1, ML_W)
    norm_f_r = norm_f.reshape(1, D_MODEL)

    x_arr, ctx_arr, ctx_block = x, ctx, 0
    for l in range(depth):
        last = l == depth - 1
        p_ml, mix_loc, k_t, g_in, g_fg = _in_proj(x_arr, ctx_arr, mods, l, norm1_r, w_main, w_kg, conv_w8,
                                                  gm_norm_r, gm_w_b, gm_bias,
                                                  seq=seq, total=total, ctx_block=ctx_block)
        ml = _mlstm(p_ml, k_t, g_in, g_fg, bias_in[l], bias_fg[l], ml_norm_r[l], wo_b[l, :ML_W],
                    n_ctx_chunks=n_ctx_chunks, ctx_out=not last)
        xs = _out_ffn(x_arr, ctx_arr, ml, mix_loc, mods, l, norm2_r, wo_b[:, ML_W:], w1_b, w3_b, w2_b, norm_f_r,
                      seq=seq, ctx_block=ctx_block, out_rows=seq if last else total,
                      tile_rows=LAST_TILE_ROWS if last else TILE_ROWS, final_norm=last)
        x_arr, ctx_arr, ctx_block = xs, xs, seq // SUB_ROWS
    return xs
```

```python
import functools

import jax
import jax.numpy as jnp
from jax import lax
from jax.experimental import pallas as pl
from jax.experimental.pallas import tpu as pltpu

D_MODEL = 1024
GRID_W = 64
ML_W = D_MODEL // 2
ML_HEADS = 4
ML_DH = ML_W // ML_HEADS
CHUNK = 128
CV_W = D_MODEL // 4
GM_W = D_MODEL // 4
GM_GROUPS = 4
N_GATES = 4 * ML_HEADS
D_FF = 2816
EPS = 1e-6
LOG2_E = 1.4426950408889634
LANES = 128
SUBLANES = 8
N_DIRHEAD = 2 * ML_HEADS
FINISH_CHUNKS = 2

_OFF_K = 0
_OFF_V = _OFF_K + ML_W
_OFF_G = _OFF_V + ML_W
_OFF_Q = _OFF_G + N_GATES
_OFF_O = _OFF_Q + ML_W
_OFF_CV = _OFF_O + ML_W
_OFF_GM = _OFF_CV + 3 * CV_W
_D_IN = _OFF_GM + 2 * GM_W

P_ML_W = 3 * ML_W
P_LOC_W = 3 * CV_W + 2 * GM_W
MIX_LOC_W = CV_W + GM_W
KG_ROWS = ML_W + N_GATES
MOD_ROWS = 24
TILE_ROWS = 768
LAST_TILE_ROWS = 1024
SUB_ROWS = 256

_VMEM_LIMIT = 56 * 1024 * 1024

_BF16 = jnp.bfloat16
_F32 = jnp.float32


def _dot(a, b):
    return jnp.dot(a, b, preferred_element_type=_F32)


def _rms_scale(x):
    return lax.rsqrt(jnp.mean(x * x, axis=-1, keepdims=True) + EPS)


def _ada_kernel(c_ref, w_ref, b_ref, o_ref):
    c = c_ref[...]
    s = (c * jax.nn.sigmoid(c)).astype(_BF16)
    o_ref[...] = _dot(s, w_ref[...].astype(_BF16)) + b_ref[...]


def _ada_table(cc, w_ada, b_ada):
    depth, _, n = w_ada.shape
    tn = 1536
    return pl.pallas_call(
        _ada_kernel,
        grid=(depth, n // tn),
        in_specs=[
            pl.BlockSpec((MOD_ROWS, D_MODEL), lambda l, j: (0, 0)),
            pl.BlockSpec((None, D_MODEL, tn), lambda l, j: (l, 0, j)),
            pl.BlockSpec((None, 1, tn), lambda l, j: (l, 0, j)),
        ],
        out_specs=pl.BlockSpec((None, MOD_ROWS, tn), lambda l, j: (l, 0, j)),
        out_shape=jax.ShapeDtypeStruct((depth, MOD_ROWS, n), _F32),
        compiler_params=pltpu.CompilerParams(vmem_limit_bytes=_VMEM_LIMIT),
        name="ada_table",
    )(cc, w_ada, b_ada.reshape(depth, 1, n))


def _skewed_order(n_sub, n_stages):
    return [(k, t - k) for t in range(n_sub + n_stages - 1) for k in range(n_stages) if 0 <= t - k < n_sub]


def _residual_specs(x_rows, tm, ctx_block):
    n_sub = tm // SUB_ROWS
    n_blocks = x_rows // SUB_ROWS

    def sub_map(b, i, *, s):
        return (b, jnp.minimum(i * n_sub + s, n_blocks - 1), 0)

    subs = [pl.BlockSpec((None, SUB_ROWS, D_MODEL), functools.partial(sub_map, s=s)) for s in range(n_sub)]
    return subs + [pl.BlockSpec((None, SUB_ROWS, D_MODEL), lambda b, i: (b, ctx_block, 0))]


def _residual_sub(x_refs, ctx_ref, s, seq):
    n_sub = len(x_refs)
    is_ctx = (pl.program_id(1) * n_sub + s) * SUB_ROWS >= seq
    return is_ctx, jnp.where(is_ctx, ctx_ref[...], x_refs[s][...])


def _mod_vec(mod_ref, is_ctx, batch_rows):
    return jnp.where(is_ctx, mod_ref[batch_rows:batch_rows + 1, :], mod_ref[pl.ds(pl.program_id(0), 1), :])


def _local_mix(p_loc, is_ctx, cw_ref, gmn_ref, gmw_ref, gmb_ref):
    sb = p_loc.shape[0]
    t_idx = lax.broadcasted_iota(jnp.int32, (sb, 1), 0)
    pos = jnp.where(is_ctx, t_idx, t_idx & (GRID_W - 1))
    last_pos = jnp.where(is_ctx, sb - 1, GRID_W - 1)
    gate_b = p_loc[:, 0:CV_W]
    z = p_loc[:, CV_W:2 * CV_W] * p_loc[:, 2 * CV_W:3 * CV_W]
    z_prev = jnp.where(pos == 0, 0.0, pltpu.roll(z, 1, 0))
    z_next = jnp.where(pos == last_pos, 0.0, pltpu.roll(z, sb - 1, 0))
    conv = gate_b * (cw_ref[0:1, :] * z_prev + cw_ref[1:2, :] * z + cw_ref[2:3, :] * z_next)
    u = p_loc[:, 3 * CV_W:3 * CV_W + GM_W]
    v = p_loc[:, 3 * CV_W + GM_W:3 * CV_W + 2 * GM_W]
    vn = (v * _rms_scale(v) * gmn_ref[...]).astype(_BF16)
    group = lax.broadcasted_iota(jnp.int32, (CHUNK, GM_W), 1) // (GM_W // GM_GROUPS)
    zero = jnp.zeros((CHUNK, GM_W), _BF16)
    zs = []
    for n in range(sb // CHUNK):
        vc = vn[n * CHUNK:(n + 1) * CHUNK]
        v_groups = jnp.concatenate([jnp.where(group == g, vc, zero) for g in range(GM_GROUPS)], axis=0)
        zs.append(gmb_ref[...] + _dot(gmw_ref[...], v_groups))
    gm = u * jnp.concatenate(zs, axis=0)
    return jnp.concatenate([conv.astype(_BF16), gm.astype(_BF16)], axis=1)


def _in_proj_kernel(*refs, n_sub, seq, batch_rows):
    x_refs, ctx_ref = refs[:n_sub], refs[n_sub]
    (sh_ref, sc_ref, nw_ref, w_ref, wkg_ref, cw_ref, gmn_ref, gmw_ref, gmb_ref,
     pml_ref, mix_ref, kt_ref, gi_ref, gf_ref) = refs[n_sub + 1:]
    sb = SUB_ROWS
    cps = sb // CHUNK
    k_scale = ML_DH ** -0.5

    def norm_stage(s):
        is_ctx, x = _residual_sub(x_refs, ctx_ref, s, seq)
        xn = x * _rms_scale(x) * nw_ref[...]
        hx = xn * (1.0 + _mod_vec(sc_ref, is_ctx, batch_rows)) + _mod_vec(sh_ref, is_ctx, batch_rows)
        return is_ctx, hx.astype(_BF16)

    def proj_stage(s, hb):
        rs = slice(s * sb, (s + 1) * sb)
        nt_dims = (((1,), (1,)), ((), ()))
        p_v = lax.dot_general(hb, w_ref[_OFF_V:_OFF_V + ML_W, :], nt_dims, preferred_element_type=_F32)
        p_rest = lax.dot_general(hb, w_ref[_OFF_Q:, :], nt_dims, preferred_element_type=_F32)
        pml_ref[rs, 0:ML_W] = p_v.astype(_BF16)
        pml_ref[rs, ML_W:] = p_rest[:, :2 * ML_W].astype(_BF16)
        kg = lax.dot_general(wkg_ref[...], hb, (((1,), (1,)), ((), ())), preferred_element_type=_F32)
        for n in range(cps):
            c = s * cps + n
            lanes = slice(n * CHUNK, (n + 1) * CHUNK)
            kt_ref[c] = (kg[:ML_W, lanes] * k_scale).astype(_BF16)
            gi_ref[c * N_DIRHEAD:(c + 1) * N_DIRHEAD, :] = kg[ML_W:ML_W + N_DIRHEAD, lanes]
            gf_ref[c * N_DIRHEAD:(c + 1) * N_DIRHEAD, :] = kg[ML_W + N_DIRHEAD:, lanes]
        return p_rest[:, 2 * ML_W:]

    def local_stage(s, is_ctx, p_loc):
        mix_ref[s * sb:(s + 1) * sb, :] = _local_mix(p_loc, is_ctx, cw_ref, gmn_ref, gmw_ref, gmb_ref)

    is_ctx, hb, p_loc = {}, {}, {}
    for k, s in _skewed_order(n_sub, 3):
        if k == 0:
            is_ctx[s], hb[s] = norm_stage(s)
        elif k == 1:
            p_loc[s] = proj_stage(s, hb.pop(s))
        else:
            local_stage(s, is_ctx.pop(s), p_loc.pop(s))


def _in_proj(x_arr, ctx_arr, mods, layer, norm_w, w_main, w_kg, conv_w, gm_norm, gm_w, gm_b,
             *, seq, total, ctx_block):
    bsz = x_arr.shape[0]
    tm = TILE_ROWS
    n_sub = tm // SUB_ROWS
    cpt = tm // CHUNK
    n_chunks = total // CHUNK
    kern = functools.partial(_in_proj_kernel, n_sub=n_sub, seq=seq, batch_rows=bsz)
    row_map = lambda b, i: (b, i, 0)
    lay = lambda *shape: pl.BlockSpec((None,) + shape, lambda b, i: (layer,) + (0,) * len(shape))
    return pl.pallas_call(
        kern,
        grid=(bsz, total // tm),
        in_specs=_residual_specs(x_arr.shape[1], tm, ctx_block) + [
            pl.BlockSpec((None, MOD_ROWS, D_MODEL), lambda b, i: (layer, 0, 0)),
            pl.BlockSpec((None, MOD_ROWS, D_MODEL), lambda b, i: (layer, 0, 1)),
            lay(1, D_MODEL),
            lay(_D_IN, D_MODEL),
            lay(KG_ROWS, D_MODEL),
            lay(SUBLANES, CV_W),
            lay(1, GM_W),
            lay(CHUNK, GM_GROUPS * CHUNK),
            lay(CHUNK, GM_W),
        ],
        out_specs=[
            pl.BlockSpec((None, tm, P_ML_W), row_map),
            pl.BlockSpec((None, tm, MIX_LOC_W), row_map),
            pl.BlockSpec((None, cpt, ML_W, CHUNK), lambda b, i: (b, i, 0, 0)),
            pl.BlockSpec((None, cpt * N_DIRHEAD, CHUNK), row_map),
            pl.BlockSpec((None, cpt * N_DIRHEAD, CHUNK), row_map),
        ],
        out_shape=[
            jax.ShapeDtypeStruct((bsz, total, P_ML_W), _BF16),
            jax.ShapeDtypeStruct((bsz, total, MIX_LOC_W), _BF16),
            jax.ShapeDtypeStruct((bsz, n_chunks, ML_W, CHUNK), _BF16),
            jax.ShapeDtypeStruct((bsz, n_chunks * N_DIRHEAD, CHUNK), _F32),
            jax.ShapeDtypeStruct((bsz, n_chunks * N_DIRHEAD, CHUNK), _F32),
        ],
        compiler_params=pltpu.CompilerParams(vmem_limit_bytes=_VMEM_LIMIT),
        name="in_proj",
    )(*([x_arr] * n_sub), ctx_arr, mods, mods, norm_w, w_main, w_kg, conv_w, gm_norm, gm_w, gm_b)


def _directional_scan(x, combine, fill):
    row = lax.broadcasted_iota(jnp.int32, x.shape, 0)
    lane = lax.broadcasted_iota(jnp.int32, x.shape, 1)
    fwd = (row & (N_DIRHEAD - 1)) < ML_HEADS
    n = x.shape[1]
    s = 1
    while s < n:
        prev = jnp.where(lane >= s, pltpu.roll(x, s, 1), fill)
        nxt = jnp.where(lane < n - s, pltpu.roll(x, n - s, 1), fill)
        x = combine(x, jnp.where(fwd, prev, nxt))
        s *= 2
    return x


def _mlstm_kernel(p_ref, kt_ref, gi_ref, gf_ref, bi_ref, bf_ref, mln_ref, wo_ref, o_ref,
                  r_s, cm_s, b_s, bsum_s, rmax_s, cn_s, rhs_s, kti_s, hf_s, hb_s,
                  *, n_chunks, n_ctx_chunks, ctx_out):
    L = CHUNK
    R = N_DIRHEAD
    n_lat = n_chunks - n_ctx_chunks
    neg_inf = -jnp.inf

    tile_rows = lambda a: jnp.concatenate([a] * n_chunks, axis=0)
    li = gi_ref[...] + tile_rows(bi_ref[...])
    lf = jax.nn.log_sigmoid(gf_ref[...] + tile_rows(bf_ref[...]))
    b = _directional_scan(lf, jnp.add, 0.0)
    r = li - b
    r_s[...] = r
    b_s[...] = b
    cm_s[...] = _directional_scan(r, jnp.maximum, neg_inf)
    bsum_s[...] = jnp.broadcast_to(jnp.sum(lf, axis=1, keepdims=True), lf.shape)
    rmax_s[...] = jnp.broadcast_to(jnp.max(r, axis=1, keepdims=True), lf.shape)

    cn_s[...] = jnp.zeros_like(cn_s)
    eye = (lax.broadcasted_iota(jnp.int32, (ML_DH, ML_DH), 0)
           == lax.broadcasted_iota(jnp.int32, (ML_DH, ML_DH), 1)).astype(_BF16)
    for j in range(R):
        rhs_s[j, 0:L, ML_DH:] = jnp.ones((L, ML_DH), _BF16)
        rhs_s[j, L:, :] = jnp.zeros((ML_DH, 2 * ML_DH), _BF16)
        kti_s[j, :, L:] = eye

    def make_step(with_out):
        def step(i, m_prev):
            cf = jnp.where(i < n_ctx_chunks, n_lat + i, i - n_ctx_chunks)
            cb = n_chunks - 1 - i
            fwd_rows = lax.broadcasted_iota(jnp.int32, (R, L), 0) < ML_HEADS
            rows_f = pl.ds(pl.multiple_of(cf * R, R), R)
            rows_b = pl.ds(pl.multiple_of(cb * R, R), R)
            pick = lambda ref: jnp.where(fwd_rows, ref[rows_f, :], ref[rows_b, :])
            r8 = pick(r_s)
            rmax = pick(rmax_s)
            a8 = jnp.maximum(m_prev, pick(cm_s))
            e_negm = jnp.exp(-(pick(b_s) + a8))
            a_last = jnp.maximum(m_prev, rmax)
            decay = jnp.exp(m_prev - a_last)
            f_kw = jnp.exp(r8 - rmax) * jnp.exp(rmax - a_last)
            m_new = pick(bsum_s) + a_last
            pad_rows = jnp.zeros((LANES - 2 * R, L), _F32)
            cols = jnp.concatenate([a8 * LOG2_E, e_negm, pad_rows], axis=0).T
            r8_l2 = r8 * LOG2_E
            m_prev_l2 = m_prev * LOG2_E
            if with_out:
                t_idx = lax.broadcasted_iota(jnp.int32, (L, L), 0)
                s_idx = lax.broadcasted_iota(jnp.int32, (L, L), 1)
                causal = (s_idx <= t_idx, s_idx >= t_idx)

            chunk_of = lambda j: cf if j < ML_HEADS else cb
            rows_of = lambda j: pl.ds(pl.multiple_of(chunk_of(j) * L, L), L)
            head_of = lambda j: slice((j % ML_HEADS) * ML_DH, (j % ML_HEADS + 1) * ML_DH)
            kts, qk_qs, s_exts, zs, upds = [], [], [], [], []
            for j in range(R):
                kt = kt_ref[chunk_of(j), head_of(j), :]
                kts.append(kt)
                rhs_s[j, 0:L, 0:ML_DH] = p_ref[rows_of(j), head_of(j)]
                if with_out:
                    kti_s[j, :, 0:L] = kt
                    h = j % ML_HEADS
                    q = p_ref[rows_of(j), ML_W + h * ML_DH:ML_W + (h + 1) * ML_DH]
                    qk_qs.append(_dot(q, kti_s[j]))
            if with_out:
                for j in range(R):
                    a_col = jnp.broadcast_to(cols[:, j:j + 1], (L, L))
                    logw = jnp.concatenate(
                        [jnp.where(causal[j // ML_HEADS], r8_l2[j:j + 1, :] - a_col, neg_inf),
                         m_prev_l2[j:j + 1, :] - a_col], axis=1)
                    s_exts.append((qk_qs[j] * jnp.exp2(logw)).astype(_BF16))
            for j in range(R):
                if with_out:
                    zs.append(_dot(s_exts[j], rhs_s[j]))
                kw = (kts[j].astype(_F32) * f_kw[j:j + 1, :]).astype(_BF16)
                upds.append(_dot(kw, rhs_s[j, 0:L, :]))
            if with_out:
                for j in range(R):
                    z = zs[j]
                    hd_s = hf_s if j < ML_HEADS else hb_s
                    floor = jnp.broadcast_to(cols[:, R + j:R + j + 1], (L, ML_DH))
                    hd_s[rows_of(j), head_of(j)] = z[:, :ML_DH] / jnp.maximum(jnp.abs(z[:, ML_DH:]), floor)
            for j in range(R):
                cn = decay[j:j + 1, 0:1] * cn_s[j] + upds[j]
                cn_s[j] = cn
                rhs_s[j, L:, :] = cn.astype(_BF16)
            return m_new
        return step

    m0 = jnp.zeros((R, L), _F32)
    m1 = lax.fori_loop(0, n_ctx_chunks, make_step(ctx_out), m0)
    lax.fori_loop(n_ctx_chunks, n_chunks, make_step(True), m1)

    fin = FINISH_CHUNKS * L

    def gate_stage(k):
        rows = slice(k * fin, (k + 1) * fin)
        heads = []
        for h in range(ML_HEADS):
            hs = slice(h * ML_DH, (h + 1) * ML_DH)
            hh = hf_s[rows, hs] + hb_s[rows, hs]
            hn = hh * _rms_scale(hh) * mln_ref[:, hs]
            og = p_ref[rows, 2 * ML_W + h * ML_DH:2 * ML_W + (h + 1) * ML_DH].astype(_F32)
            o = 0.5 * jnp.tanh(0.5 * og) + 0.5
            heads.append((o * hn).astype(_BF16))
        return jnp.concatenate(heads, axis=1)

    def proj_stage(k, gated):
        o_ref[k * fin:(k + 1) * fin, :] = _dot(gated, wo_ref[...]).astype(_BF16)

    assert n_chunks % FINISH_CHUNKS == 0 and n_lat % FINISH_CHUNKS == 0
    gated = {}
    for stage, k in _skewed_order((n_chunks if ctx_out else n_lat) // FINISH_CHUNKS, 2):
        if stage == 0:
            gated[k] = gate_stage(k)
        else:
            proj_stage(k, gated.pop(k))
    if not ctx_out:
        o_ref[n_lat * L:, :] = jnp.zeros((n_ctx_chunks * L, D_MODEL), _BF16)


def _mlstm(p_ml, k_t, g_in, g_fg, bias_in, bias_fg, ml_norm, w_out_ml, *, n_ctx_chunks, ctx_out):
    bsz, t, _ = p_ml.shape
    n_chunks = t // CHUNK
    kern = functools.partial(_mlstm_kernel, n_chunks=n_chunks, n_ctx_chunks=n_ctx_chunks, ctx_out=ctx_out)
    gate_rows = n_chunks * N_DIRHEAD
    gate_scratch = pltpu.VMEM((gate_rows, CHUNK), _F32)
    return pl.pallas_call(
        kern,
        grid=(bsz,),
        in_specs=[
            pl.BlockSpec((None, t, P_ML_W), lambda b: (b, 0, 0)),
            pl.BlockSpec((None, n_chunks, ML_W, CHUNK), lambda b: (b, 0, 0, 0)),
            pl.BlockSpec((None, gate_rows, CHUNK), lambda b: (b, 0, 0)),
            pl.BlockSpec((None, gate_rows, CHUNK), lambda b: (b, 0, 0)),
            pl.BlockSpec((N_DIRHEAD, CHUNK), lambda b: (0, 0)),
            pl.BlockSpec((N_DIRHEAD, CHUNK), lambda b: (0, 0)),
            pl.BlockSpec((1, ML_W), lambda b: (0, 0)),
            pl.BlockSpec((ML_W, D_MODEL), lambda b: (0, 0)),
        ],
        out_specs=pl.BlockSpec((None, t, D_MODEL), lambda b: (b, 0, 0)),
        out_shape=jax.ShapeDtypeStruct((bsz, t, D_MODEL), _BF16),
        scratch_shapes=[
            gate_scratch,
            gate_scratch,
            gate_scratch,
            gate_scratch,
            gate_scratch,
            pltpu.VMEM((N_DIRHEAD, ML_DH, 2 * ML_DH), _F32),
            pltpu.VMEM((N_DIRHEAD, CHUNK + ML_DH, 2 * ML_DH), _BF16),
            pltpu.VMEM((N_DIRHEAD, ML_DH, CHUNK + ML_DH), _BF16),
            pltpu.VMEM((t, ML_W), _F32),
            pltpu.VMEM((t, ML_W), _F32),
        ],
        compiler_params=pltpu.CompilerParams(vmem_limit_bytes=_VMEM_LIMIT),
        name="mlstm",
    )(p_ml, k_t, g_in, g_fg, bias_in, bias_fg, ml_norm, w_out_ml)


def _out_ffn_kernel(*refs, n_sub, seq, batch_rows, final_norm):
    x_refs, ctx_ref = refs[:n_sub], refs[n_sub]
    (ml_ref, mix_ref, g1_ref, sh2_ref, sc2_ref, g2_ref, n2_ref, wo_ref, w1_ref, w3_ref, w2_ref, nf_ref,
     o_ref) = refs[n_sub + 1:]
    sb = SUB_ROWS

    def proj_stage(s):
        rs = slice(s * sb, (s + 1) * sb)
        is_ctx, x = _residual_sub(x_refs, ctx_ref, s, seq)
        proj = ml_ref[rs, :].astype(_F32) + _dot(mix_ref[rs, :], wo_ref[...])
        x1 = x + _mod_vec(g1_ref, is_ctx, batch_rows) * proj
        hx2 = x1 * _rms_scale(x1) * n2_ref[...]
        hx2 = hx2 * (1.0 + _mod_vec(sc2_ref, is_ctx, batch_rows)) + _mod_vec(sh2_ref, is_ctx, batch_rows)
        return is_ctx, x1, hx2.astype(_BF16)

    def up_stage(hx2):
        a = _dot(hx2, w1_ref[...])
        return (a * jax.nn.sigmoid(a) * _dot(hx2, w3_ref[...])).astype(_BF16)

    def down_stage(s, is_ctx, x1, act):
        x2 = x1 + _mod_vec(g2_ref, is_ctx, batch_rows) * _dot(act, w2_ref[...])
        if final_norm:
            x2 = x2 * _rms_scale(x2) * nf_ref[...]
        o_ref[s * sb:(s + 1) * sb, :] = x2

    is_ctx, x1, hx2, act = {}, {}, {}, {}
    for k, s in _skewed_order(n_sub, 3):
        if k == 0:
            is_ctx[s], x1[s], hx2[s] = proj_stage(s)
        elif k == 1:
            act[s] = up_stage(hx2.pop(s))
        else:
            down_stage(s, is_ctx.pop(s), x1.pop(s), act.pop(s))


def _out_ffn(x_arr, ctx_arr, ml, mix_loc, mods, layer, norm2, w_out, w1, w3, w2, norm_f,
             *, seq, ctx_block, out_rows, tile_rows, final_norm):
    bsz = x_arr.shape[0]
    tm = tile_rows
    n_sub = tm // SUB_ROWS
    kern = functools.partial(_out_ffn_kernel, n_sub=n_sub, seq=seq, batch_rows=bsz, final_norm=final_norm)
    row_map = lambda b, i: (b, i, 0)
    mod_spec = lambda k: pl.BlockSpec((None, MOD_ROWS, D_MODEL), lambda b, i: (layer, 0, k))
    lay = lambda *shape: pl.BlockSpec((None,) + shape, lambda b, i: (layer,) + (0,) * len(shape))
    return pl.pallas_call(
        kern,
        grid=(bsz, out_rows // tm),
        in_specs=_residual_specs(x_arr.shape[1], tm, ctx_block) + [
            pl.BlockSpec((None, tm, D_MODEL), row_map),
            pl.BlockSpec((None, tm, MIX_LOC_W), row_map),
            mod_spec(2), mod_spec(3), mod_spec(4), mod_spec(5),
            lay(1, D_MODEL),
            lay(MIX_LOC_W, D_MODEL),
            lay(D_MODEL, D_FF),
            lay(D_MODEL, D_FF),
            lay(D_FF, D_MODEL),
            pl.BlockSpec((1, D_MODEL), lambda b, i: (0, 0)),
        ],
        out_specs=pl.BlockSpec((None, tm, D_MODEL), row_map),
        out_shape=jax.ShapeDtypeStruct((bsz, out_rows, D_MODEL), _F32),
        compiler_params=pltpu.CompilerParams(vmem_limit_bytes=_VMEM_LIMIT),
        name="out_ffn",
    )(*([x_arr] * n_sub), ctx_arr, ml, mix_loc, mods, mods, mods, mods, norm2, w_out, w1, w3, w2, norm_f)


def _prepare_in_weights(w_in, b_gates):
    w_rows = jnp.swapaxes(w_in, 1, 2).astype(_BF16)
    kinds = (0, 2, 1, 3)
    gate_rows = [w_rows[:, _OFF_G + kind * ML_HEADS:_OFF_G + (kind + 1) * ML_HEADS] for kind in kinds]
    w_kg = jnp.concatenate([w_rows[:, _OFF_K:_OFF_K + ML_W]] + gate_rows, axis=1)
    bias = jnp.concatenate([b_gates[:, kind * ML_HEADS:(kind + 1) * ML_HEADS] for kind in kinds], axis=-1)
    bias = jnp.broadcast_to(bias.astype(_F32)[:, :, None], bias.shape + (CHUNK,))
    return w_rows, w_kg, bias[:, :N_DIRHEAD], bias[:, N_DIRHEAD:]


def kernel(x, c, ctx, c_ctx, w_ada, b_ada, norm1, norm2, w_in, b_gates, ml_norm, conv_w, gm_norm, gm_ws,
           gm_bs, w_out, w1, w3, w2, norm_f):
    bsz, seq, _ = x.shape
    ctx_len = ctx.shape[1]
    depth = w_in.shape[0]
    total = seq + ctx_len
    assert bsz < MOD_ROWS and w_in.shape[-1] == _D_IN and w1.shape[-1] == D_FF
    assert seq % GRID_W == 0 and seq % CHUNK == 0 and ctx_len % CHUNK == 0
    assert total % TILE_ROWS == 0 and TILE_ROWS % CHUNK == 0 and total - TILE_ROWS <= seq
    assert seq % LAST_TILE_ROWS == 0 and TILE_ROWS % SUB_ROWS == 0 and LAST_TILE_ROWS % SUB_ROWS == 0
    assert seq % SUB_ROWS == 0 and ctx_len == SUB_ROWS and SUB_ROWS % CHUNK == 0 and SUB_ROWS % GRID_W == 0
    n_ctx_chunks = ctx_len // CHUNK

    cc = jnp.concatenate([c, c_ctx[None, :], jnp.zeros((MOD_ROWS - bsz - 1, D_MODEL), _F32)], axis=0)
    mods = _ada_table(cc, w_ada, b_ada)

    w_main, w_kg, bias_in, bias_fg = _prepare_in_weights(w_in, b_gates)
    wo_b, w1_b, w3_b, w2_b = (w.astype(_BF16) for w in (w_out, w1, w3, w2))
    gm_w_b = jnp.swapaxes(gm_ws, 1, 2).reshape(depth, CHUNK, GM_GROUPS * CHUNK).astype(_BF16)
    gm_bias = jnp.repeat(jnp.swapaxes(gm_bs, 1, 2), GM_W // GM_GROUPS, axis=2)
    conv_w8 = jnp.pad(conv_w, ((0, 0), (0, SUBLANES - conv_w.shape[1]), (0, 0)))
    norm1_r = norm1.reshape(depth, 1, D_MODEL)
    norm2_r = norm2.reshape(depth, 1, D_MODEL)
    gm_norm_r = gm_norm.reshape(depth, 1, GM_W)
    ml_norm_r = ml_norm.reshape(depth, 1, ML_W)
    norm_f_r = norm_f.reshape(1, D_MODEL)

    x_arr, ctx_arr, ctx_block = x, ctx, 0
    for l in range(depth):
        last = l == depth - 1
        p_ml, mix_loc, k_t, g_in, g_fg = _in_proj(x_arr, ctx_arr, mods, l, norm1_r, w_main, w_kg, conv_w8,
                                                  gm_norm_r, gm_w_b, gm_bias,
                                                  seq=seq, total=total, ctx_block=ctx_block)
        ml = _mlstm(p_ml, k_t, g_in, g_fg, bias_in[l], bias_fg[l], ml_norm_r[l], wo_b[l, :ML_W],
                    n_ctx_chunks=n_ctx_chunks, ctx_out=not last)
        xs = _out_ffn(x_arr, ctx_arr, ml, mix_loc, mods, l, norm2_r, wo_b[:, ML_W:], w1_b, w3_b, w2_b, norm_f_r,
                      seq=seq, ctx_block=ctx_block, out_rows=seq if last else total,
                      tile_rows=LAST_TILE_ROWS if last else TILE_ROWS, final_norm=last)
        x_arr, ctx_arr, ctx_block = xs, xs, seq // SUB_ROWS
    return xs
```

```python
import functools

import jax
import jax.numpy as jnp
from jax import lax
from jax.experimental import pallas as pl
from jax.experimental.pallas import tpu as pltpu

D_MODEL = 1024
GRID_W = 64
ML_W = D_MODEL // 2
ML_HEADS = 4
ML_DH = ML_W // ML_HEADS
CHUNK = 128
CV_W = D_MODEL // 4
GM_W = D_MODEL // 4
GM_GROUPS = 4
N_GATES = 4 * ML_HEADS
D_FF = 2816
EPS = 1e-6
LOG2_E = 1.4426950408889634
LANES = 128
SUBLANES = 8
N_DIRHEAD = 2 * ML_HEADS
FINISH_CHUNKS = 2

_OFF_K = 0
_OFF_V = _OFF_K + ML_W
_OFF_G = _OFF_V + ML_W
_OFF_Q = _OFF_G + N_GATES
_OFF_O = _OFF_Q + ML_W
_OFF_CV = _OFF_O + ML_W
_OFF_GM = _OFF_CV + 3 * CV_W
_D_IN = _OFF_GM + 2 * GM_W

P_ML_W = 3 * ML_W
P_LOC_W = 3 * CV_W + 2 * GM_W
MIX_LOC_W = CV_W + GM_W
KG_ROWS = ML_W + N_GATES
MOD_ROWS = 24
TILE_ROWS = 768
LAST_TILE_ROWS = 1024
SUB_ROWS = 256

_VMEM_LIMIT = 56 * 1024 * 1024

_BF16 = jnp.bfloat16
_F32 = jnp.float32


def _dot(a, b):
    return jnp.dot(a, b, preferred_element_type=_F32)


def _rms_scale(x):
    return lax.rsqrt(jnp.mean(x * x, axis=-1, keepdims=True) + EPS)


def _ada_kernel(c_ref, w_ref, b_ref, o_ref):
    c = c_ref[...]
    s = (c * jax.nn.sigmoid(c)).astype(_BF16)
    o_ref[...] = _dot(s, w_ref[...].astype(_BF16)) + b_ref[...]


def _ada_table(cc, w_ada, b_ada):
    depth, _, n = w_ada.shape
    tn = 1536
    return pl.pallas_call(
        _ada_kernel,
        grid=(depth, n // tn),
        in_specs=[
            pl.BlockSpec((MOD_ROWS, D_MODEL), lambda l, j: (0, 0)),
            pl.BlockSpec((None, D_MODEL, tn), lambda l, j: (l, 0, j)),
            pl.BlockSpec((None, 1, tn), lambda l, j: (l, 0, j)),
        ],
        out_specs=pl.BlockSpec((None, MOD_ROWS, tn), lambda l, j: (l, 0, j)),
        out_shape=jax.ShapeDtypeStruct((depth, MOD_ROWS, n), _F32),
        compiler_params=pltpu.CompilerParams(vmem_limit_bytes=_VMEM_LIMIT),
        name="ada_table",
    )(cc, w_ada, b_ada.reshape(depth, 1, n))


def _skewed_order(n_sub, n_stages):
    return [(k, t - k) for t in range(n_sub + n_stages - 1) for k in range(n_stages) if 0 <= t - k < n_sub]


def _residual_specs(x_rows, tm, ctx_block):
    n_sub = tm // SUB_ROWS
    n_blocks = x_rows // SUB_ROWS

    def sub_map(b, i, *, s):
        return (b, jnp.minimum(i * n_sub + s, n_blocks - 1), 0)

    subs = [pl.BlockSpec((None, SUB_ROWS, D_MODEL), functools.partial(sub_map, s=s)) for s in range(n_sub)]
    return subs + [pl.BlockSpec((None, SUB_ROWS, D_MODEL), lambda b, i: (b, ctx_block, 0))]


def _residual_sub(x_refs, ctx_ref, s, seq):
    n_sub = len(x_refs)
    is_ctx = (pl.program_id(1) * n_sub + s) * SUB_ROWS >= seq
    return is_ctx, jnp.where(is_ctx, ctx_ref[...], x_refs[s][...])


def _mod_vec(mod_ref, is_ctx, batch_rows):
    return jnp.where(is_ctx, mod_ref[batch_rows:batch_rows + 1, :], mod_ref[pl.ds(pl.program_id(0), 1), :])


def _local_mix(p_loc, is_ctx, cw_ref, gmn_ref, gmw_ref, gmb_ref):
    sb = p_loc.shape[0]
    t_idx = lax.broadcasted_iota(jnp.int32, (sb, 1), 0)
    pos = jnp.where(is_ctx, t_idx, t_idx & (GRID_W - 1))
    last_pos = jnp.where(is_ctx, sb - 1, GRID_W - 1)
    gate_b = p_loc[:, 0:CV_W]
    z = p_loc[:, CV_W:2 * CV_W] * p_loc[:, 2 * CV_W:3 * CV_W]
    z_prev = jnp.where(pos == 0, 0.0, pltpu.roll(z, 1, 0))
    z_next = jnp.where(pos == last_pos, 0.0, pltpu.roll(z, sb - 1, 0))
    conv = gate_b * (cw_ref[0:1, :] * z_prev + cw_ref[1:2, :] * z + cw_ref[2:3, :] * z_next)
    u = p_loc[:, 3 * CV_W:3 * CV_W + GM_W]
    v = p_loc[:, 3 * CV_W + GM_W:3 * CV_W + 2 * GM_W]
    vn = (v * _rms_scale(v) * gmn_ref[...]).astype(_BF16)
    group = lax.broadcasted_iota(jnp.int32, (CHUNK, GM_W), 1) // (GM_W // GM_GROUPS)
    zero = jnp.zeros((CHUNK, GM_W), _BF16)
    zs = []
    for n in range(sb // CHUNK):
        vc = vn[n * CHUNK:(n + 1) * CHUNK]
        v_groups = jnp.concatenate([jnp.where(group == g, vc, zero) for g in range(GM_GROUPS)], axis=0)
        zs.append(gmb_ref[...] + _dot(gmw_ref[...], v_groups))
    gm = u * jnp.concatenate(zs, axis=0)
    return jnp.concatenate([conv.astype(_BF16), gm.astype(_BF16)], axis=1)


def _in_proj_kernel(*refs, n_sub, seq, batch_rows):
    x_refs, ctx_ref = refs[:n_sub], refs[n_sub]
    (sh_ref, sc_ref, nw_ref, w_ref, wkg_ref, cw_ref, gmn_ref, gmw_ref, gmb_ref,
     pml_ref, mix_ref, kt_ref, gi_ref, gf_ref, w_cols) = refs[n_sub + 1:]
    sb = SUB_ROWS

    @pl.when((pl.program_id(0) == 0) & (pl.program_id(1) == 0))
    def _():
        piece = SUB_ROWS
        sources = list(range(_OFF_V, _OFF_V + ML_W, piece)) + list(range(_OFF_Q, _D_IN, piece))
        for n, r0 in enumerate(sources):
            w_cols[:, n * piece:(n + 1) * piece] = w_ref[r0:r0 + piece, :].astype(_F32).T.astype(_BF16)
    cps = sb // CHUNK
    k_scale = ML_DH ** -0.5

    def norm_stage(s):
        is_ctx, x = _residual_sub(x_refs, ctx_ref, s, seq)
        xn = x * _rms_scale(x) * nw_ref[...]
        hx = xn * (1.0 + _mod_vec(sc_ref, is_ctx, batch_rows)) + _mod_vec(sh_ref, is_ctx, batch_rows)
        return is_ctx, hx.astype(_BF16)

    def proj_stage(s, hb):
        rs = slice(s * sb, (s + 1) * sb)
        p = _dot(hb, w_cols[...])
        pml_ref[rs, :] = p[:, :P_ML_W].astype(_BF16)
        kg = lax.dot_general(wkg_ref[...], hb, (((1,), (1,)), ((), ())), preferred_element_type=_F32)
        for n in range(cps):
            c = s * cps + n
            lanes = slice(n * CHUNK, (n + 1) * CHUNK)
            kt_ref[c] = (kg[:ML_W, lanes] * k_scale).astype(_BF16)
            gi_ref[c * N_DIRHEAD:(c + 1) * N_DIRHEAD, :] = kg[ML_W:ML_W + N_DIRHEAD, lanes]
            gf_ref[c * N_DIRHEAD:(c + 1) * N_DIRHEAD, :] = kg[ML_W + N_DIRHEAD:, lanes]
        return p[:, P_ML_W:]

    def local_stage(s, is_ctx, p_loc):
        mix_ref[s * sb:(s + 1) * sb, :] = _local_mix(p_loc, is_ctx, cw_ref, gmn_ref, gmw_ref, gmb_ref)

    is_ctx, hb, p_loc = {}, {}, {}
    for k, s in _skewed_order(n_sub, 3):
        if k == 0:
            is_ctx[s], hb[s] = norm_stage(s)
        elif k == 1:
            p_loc[s] = proj_stage(s, hb.pop(s))
        else:
            local_stage(s, is_ctx.pop(s), p_loc.pop(s))


def _in_proj(x_arr, ctx_arr, mods, layer, norm_w, w_main, w_kg, conv_w, gm_norm, gm_w, gm_b,
             *, seq, total, ctx_block):
    bsz = x_arr.shape[0]
    tm = TILE_ROWS
    n_sub = tm // SUB_ROWS
    cpt = tm // CHUNK
    n_chunks = total // CHUNK
    kern = functools.partial(_in_proj_kernel, n_sub=n_sub, seq=seq, batch_rows=bsz)
    row_map = lambda b, i: (b, i, 0)
    lay = lambda *shape: pl.BlockSpec((None,) + shape, lambda b, i: (layer,) + (0,) * len(shape))
    return pl.pallas_call(
        kern,
        grid=(bsz, total // tm),
        in_specs=_residual_specs(x_arr.shape[1], tm, ctx_block) + [
            pl.BlockSpec((None, MOD_ROWS, D_MODEL), lambda b, i: (layer, 0, 0)),
            pl.BlockSpec((None, MOD_ROWS, D_MODEL), lambda b, i: (layer, 0, 1)),
            lay(1, D_MODEL),
            lay(_D_IN, D_MODEL),
            lay(KG_ROWS, D_MODEL),
            lay(SUBLANES, CV_W),
            lay(1, GM_W),
            lay(CHUNK, GM_GROUPS * CHUNK),
            lay(CHUNK, GM_W),
        ],
        out_specs=[
            pl.BlockSpec((None, tm, P_ML_W), row_map),
            pl.BlockSpec((None, tm, MIX_LOC_W), row_map),
            pl.BlockSpec((None, cpt, ML_W, CHUNK), lambda b, i: (b, i, 0, 0)),
            pl.BlockSpec((None, cpt * N_DIRHEAD, CHUNK), row_map),
            pl.BlockSpec((None, cpt * N_DIRHEAD, CHUNK), row_map),
        ],
        out_shape=[
            jax.ShapeDtypeStruct((bsz, total, P_ML_W), _BF16),
            jax.ShapeDtypeStruct((bsz, total, MIX_LOC_W), _BF16),
            jax.ShapeDtypeStruct((bsz, n_chunks, ML_W, CHUNK), _BF16),
            jax.ShapeDtypeStruct((bsz, n_chunks * N_DIRHEAD, CHUNK), _F32),
            jax.ShapeDtypeStruct((bsz, n_chunks * N_DIRHEAD, CHUNK), _F32),
        ],
        scratch_shapes=[pltpu.VMEM((D_MODEL, P_ML_W + P_LOC_W), _BF16)],
        compiler_params=pltpu.CompilerParams(vmem_limit_bytes=_VMEM_LIMIT),
        name="in_proj",
    )(*([x_arr] * n_sub), ctx_arr, mods, mods, norm_w, w_main, w_kg, conv_w, gm_norm, gm_w, gm_b)


def _directional_scan(x, combine, fill):
    row = lax.broadcasted_iota(jnp.int32, x.shape, 0)
    lane = lax.broadcasted_iota(jnp.int32, x.shape, 1)
    fwd = (row & (N_DIRHEAD - 1)) < ML_HEADS
    n = x.shape[1]
    s = 1
    while s < n:
        prev = jnp.where(lane >= s, pltpu.roll(x, s, 1), fill)
        nxt = jnp.where(lane < n - s, pltpu.roll(x, n - s, 1), fill)
        x = combine(x, jnp.where(fwd, prev, nxt))
        s *= 2
    return x


def _mlstm_kernel(p_ref, kt_ref, gi_ref, gf_ref, bi_ref, bf_ref, mln_ref, wo_ref, o_ref,
                  r_s, cm_s, b_s, bsum_s, rmax_s, cn_s, rhs_s, kti_s, hf_s, hb_s,
                  *, n_chunks, n_ctx_chunks, ctx_out):
    L = CHUNK
    R = N_DIRHEAD
    n_lat = n_chunks - n_ctx_chunks
    neg_inf = -jnp.inf

    tile_rows = lambda a: jnp.concatenate([a] * n_chunks, axis=0)
    li = gi_ref[...] + tile_rows(bi_ref[...])
    lf = jax.nn.log_sigmoid(gf_ref[...] + tile_rows(bf_ref[...]))
    b = _directional_scan(lf, jnp.add, 0.0)
    r = li - b
    r_s[...] = r
    b_s[...] = b
    cm_s[...] = _directional_scan(r, jnp.maximum, neg_inf)
    bsum_s[...] = jnp.broadcast_to(jnp.sum(lf, axis=1, keepdims=True), lf.shape)
    rmax_s[...] = jnp.broadcast_to(jnp.max(r, axis=1, keepdims=True), lf.shape)

    cn_s[...] = jnp.zeros_like(cn_s)
    eye = (lax.broadcasted_iota(jnp.int32, (ML_DH, ML_DH), 0)
           == lax.broadcasted_iota(jnp.int32, (ML_DH, ML_DH), 1)).astype(_BF16)
    for j in range(R):
        rhs_s[j, 0:L, ML_DH:] = jnp.ones((L, ML_DH), _BF16)
        rhs_s[j, L:, :] = jnp.zeros((ML_DH, 2 * ML_DH), _BF16)
        kti_s[j, :, L:] = eye

    def make_step(with_out):
        def step(i, m_prev):
            cf = jnp.where(i < n_ctx_chunks, n_lat + i, i - n_ctx_chunks)
            cb = n_chunks - 1 - i
            fwd_rows = lax.broadcasted_iota(jnp.int32, (R, L), 0) < ML_HEADS
            rows_f = pl.ds(pl.multiple_of(cf * R, R), R)
            rows_b = pl.ds(pl.multiple_of(cb * R, R), R)
            pick = lambda ref: jnp.where(fwd_rows, ref[rows_f, :], ref[rows_b, :])
            r8 = pick(r_s)
            rmax = pick(rmax_s)
            a8 = jnp.maximum(m_prev, pick(cm_s))
            e_negm = jnp.exp(-(pick(b_s) + a8))
            a_last = jnp.maximum(m_prev, rmax)
            decay = jnp.exp(m_prev - a_last)
            f_kw = jnp.exp(r8 - rmax) * jnp.exp(rmax - a_last)
            m_new = pick(bsum_s) + a_last
            pad_rows = jnp.zeros((LANES - 2 * R, L), _F32)
            cols = jnp.concatenate([a8 * LOG2_E, e_negm, pad_rows], axis=0).T
            r8_l2 = r8 * LOG2_E
            m_prev_l2 = m_prev * LOG2_E
            if with_out:
                t_idx = lax.broadcasted_iota(jnp.int32, (L, L), 0)
                s_idx = lax.broadcasted_iota(jnp.int32, (L, L), 1)
                causal = (s_idx <= t_idx, s_idx >= t_idx)

            chunk_of = lambda j: cf if j < ML_HEADS else cb
            rows_of = lambda j: pl.ds(pl.multiple_of(chunk_of(j) * L, L), L)
            head_of = lambda j: slice((j % ML_HEADS) * ML_DH, (j % ML_HEADS + 1) * ML_DH)
            kts, qk_qs, s_exts, zs, upds = [], [], [], [], []
            for j in range(R):
                kt = kt_ref[chunk_of(j), head_of(j), :]
                kts.append(kt)
                rhs_s[j, 0:L, 0:ML_DH] = p_ref[rows_of(j), head_of(j)]
                if with_out:
                    kti_s[j, :, 0:L] = kt
                    h = j % ML_HEADS
                    q = p_ref[rows_of(j), ML_W + h * ML_DH:ML_W + (h + 1) * ML_DH]
                    qk_qs.append(_dot(q, kti_s[j]))
            if with_out:
                for j in range(R):
                    a_col = jnp.broadcast_to(cols[:, j:j + 1], (L, L))
                    logw = jnp.concatenate(
                        [jnp.where(causal[j // ML_HEADS], r8_l2[j:j + 1, :] - a_col, neg_inf),
                         m_prev_l2[j:j + 1, :] - a_col], axis=1)
                    s_exts.append((qk_qs[j] * jnp.exp2(logw)).astype(_BF16))
            for j in range(R):
                if with_out:
                    zs.append(_dot(s_exts[j], rhs_s[j]))
                kw = (kts[j].astype(_F32) * f_kw[j:j + 1, :]).astype(_BF16)
                upds.append(_dot(kw, rhs_s[j, 0:L, :]))
            if with_out:
                for j in range(R):
                    z = zs[j]
                    hd_s = hf_s if j < ML_HEADS else hb_s
                    floor = jnp.broadcast_to(cols[:, R + j:R + j + 1], (L, ML_DH))
                    hd_s[rows_of(j), head_of(j)] = z[:, :ML_DH] / jnp.maximum(jnp.abs(z[:, ML_DH:]), floor)
            for j in range(R):
                cn = decay[j:j + 1, 0:1] * cn_s[j] + upds[j]
                cn_s[j] = cn
                rhs_s[j, L:, :] = cn.astype(_BF16)
            return m_new
        return step

    m0 = jnp.zeros((R, L), _F32)
    m1 = lax.fori_loop(0, n_ctx_chunks, make_step(ctx_out), m0)
    lax.fori_loop(n_ctx_chunks, n_chunks, make_step(True), m1)

    fin = FINISH_CHUNKS * L

    def gate_stage(k):
        rows = slice(k * fin, (k + 1) * fin)
        heads = []
        for h in range(ML_HEADS):
            hs = slice(h * ML_DH, (h + 1) * ML_DH)
            hh = hf_s[rows, hs] + hb_s[rows, hs]
            hn = hh * _rms_scale(hh) * mln_ref[:, hs]
            og = p_ref[rows, 2 * ML_W + h * ML_DH:2 * ML_W + (h + 1) * ML_DH].astype(_F32)
            o = 0.5 * jnp.tanh(0.5 * og) + 0.5
            heads.append((o * hn).astype(_BF16))
        return jnp.concatenate(heads, axis=1)

    def proj_stage(k, gated):
        o_ref[k * fin:(k + 1) * fin, :] = _dot(gated, wo_ref[...]).astype(_BF16)

    assert n_chunks % FINISH_CHUNKS == 0 and n_lat % FINISH_CHUNKS == 0
    gated = {}
    for stage, k in _skewed_order((n_chunks if ctx_out else n_lat) // FINISH_CHUNKS, 2):
        if stage == 0:
            gated[k] = gate_stage(k)
        else:
            proj_stage(k, gated.pop(k))
    if not ctx_out:
        o_ref[n_lat * L:, :] = jnp.zeros((n_ctx_chunks * L, D_MODEL), _BF16)


def _mlstm(p_ml, k_t, g_in, g_fg, bias_in, bias_fg, ml_norm, w_out_ml, *, n_ctx_chunks, ctx_out):
    bsz, t, _ = p_ml.shape
    n_chunks = t // CHUNK
    kern = functools.partial(_mlstm_kernel, n_chunks=n_chunks, n_ctx_chunks=n_ctx_chunks, ctx_out=ctx_out)
    gate_rows = n_chunks * N_DIRHEAD
    gate_scratch = pltpu.VMEM((gate_rows, CHUNK), _F32)
    return pl.pallas_call(
        kern,
        grid=(bsz,),
        in_specs=[
            pl.BlockSpec((None, t, P_ML_W), lambda b: (b, 0, 0)),
            pl.BlockSpec((None, n_chunks, ML_W, CHUNK), lambda b: (b, 0, 0, 0)),
            pl.BlockSpec((None, gate_rows, CHUNK), lambda b: (b, 0, 0)),
            pl.BlockSpec((None, gate_rows, CHUNK), lambda b: (b, 0, 0)),
            pl.BlockSpec((N_DIRHEAD, CHUNK), lambda b: (0, 0)),
            pl.BlockSpec((N_DIRHEAD, CHUNK), lambda b: (0, 0)),
            pl.BlockSpec((1, ML_W), lambda b: (0, 0)),
            pl.BlockSpec((ML_W, D_MODEL), lambda b: (0, 0)),
        ],
        out_specs=pl.BlockSpec((None, t, D_MODEL), lambda b: (b, 0, 0)),
        out_shape=jax.ShapeDtypeStruct((bsz, t, D_MODEL), _BF16),
        scratch_shapes=[
            gate_scratch,
            gate_scratch,
            gate_scratch,
            gate_scratch,
            gate_scratch,
            pltpu.VMEM((N_DIRHEAD, ML_DH, 2 * ML_DH), _F32),
            pltpu.VMEM((N_DIRHEAD, CHUNK + ML_DH, 2 * ML_DH), _BF16),
            pltpu.VMEM((N_DIRHEAD, ML_DH, CHUNK + ML_DH), _BF16),
            pltpu.VMEM((t, ML_W), _F32),
            pltpu.VMEM((t, ML_W), _F32),
        ],
        compiler_params=pltpu.CompilerParams(vmem_limit_bytes=_VMEM_LIMIT),
        name="mlstm",
    )(p_ml, k_t, g_in, g_fg, bias_in, bias_fg, ml_norm, w_out_ml)


def _out_ffn_kernel(*refs, n_sub, seq, batch_rows, final_norm):
    x_refs, ctx_ref = refs[:n_sub], refs[n_sub]
    (ml_ref, mix_ref, g1_ref, sh2_ref, sc2_ref, g2_ref, n2_ref, wo_ref, w1_ref, w3_ref, w2_ref, nf_ref,
     o_ref) = refs[n_sub + 1:]
    sb = SUB_ROWS

    def proj_stage(s):
        rs = slice(s * sb, (s + 1) * sb)
        is_ctx, x = _residual_sub(x_refs, ctx_ref, s, seq)
        proj = ml_ref[rs, :].astype(_F32) + _dot(mix_ref[rs, :], wo_ref[...])
        x1 = x + _mod_vec(g1_ref, is_ctx, batch_rows) * proj
        hx2 = x1 * _rms_scale(x1) * n2_ref[...]
        hx2 = hx2 * (1.0 + _mod_vec(sc2_ref, is_ctx, batch_rows)) + _mod_vec(sh2_ref, is_ctx, batch_rows)
        return is_ctx, x1, hx2.astype(_BF16)

    def up_stage(hx2):
        a = _dot(hx2, w1_ref[...])
        return (a * jax.nn.sigmoid(a) * _dot(hx2, w3_ref[...])).astype(_BF16)

    def down_stage(s, is_ctx, x1, act):
        x2 = x1 + _mod_vec(g2_ref, is_ctx, batch_rows) * _dot(act, w2_ref[...])
        if final_norm:
            x2 = x2 * _rms_scale(x2) * nf_ref[...]
        o_ref[s * sb:(s + 1) * sb, :] = x2

    is_ctx, x1, hx2, act = {}, {}, {}, {}
    for k, s in _skewed_order(n_sub, 3):
        if k == 0:
            is_ctx[s], x1[s], hx2[s] = proj_stage(s)
        elif k == 1:
            act[s] = up_stage(hx2.pop(s))
        else:
            down_stage(s, is_ctx.pop(s), x1.pop(s), act.pop(s))


def _out_ffn(x_arr, ctx_arr, ml, mix_loc, mods, layer, norm2, w_out, w1, w3, w2, norm_f,
             *, seq, ctx_block, out_rows, tile_rows, final_norm):
    bsz = x_arr.shape[0]
    tm = tile_rows
    n_sub = tm // SUB_ROWS
    kern = functools.partial(_out_ffn_kernel, n_sub=n_sub, seq=seq, batch_rows=bsz, final_norm=final_norm)
    row_map = lambda b, i: (b, i, 0)
    mod_spec = lambda k: pl.BlockSpec((None, MOD_ROWS, D_MODEL), lambda b, i: (layer, 0, k))
    lay = lambda *shape: pl.BlockSpec((None,) + shape, lambda b, i: (layer,) + (0,) * len(shape))
    return pl.pallas_call(
        kern,
        grid=(bsz, out_rows // tm),
        in_specs=_residual_specs(x_arr.shape[1], tm, ctx_block) + [
            pl.BlockSpec((None, tm, D_MODEL), row_map),
            pl.BlockSpec((None, tm, MIX_LOC_W), row_map),
            mod_spec(2), mod_spec(3), mod_spec(4), mod_spec(5),
            lay(1, D_MODEL),
            lay(MIX_LOC_W, D_MODEL),
            lay(D_MODEL, D_FF),
            lay(D_MODEL, D_FF),
            lay(D_FF, D_MODEL),
            pl.BlockSpec((1, D_MODEL), lambda b, i: (0, 0)),
        ],
        out_specs=pl.BlockSpec((None, tm, D_MODEL), row_map),
        out_shape=jax.ShapeDtypeStruct((bsz, out_rows, D_MODEL), _F32),
        compiler_params=pltpu.CompilerParams(vmem_limit_bytes=_VMEM_LIMIT),
        name="out_ffn",
    )(*([x_arr] * n_sub), ctx_arr, ml, mix_loc, mods, mods, mods, mods, norm2, w_out, w1, w3, w2, norm_f)


def _prepare_in_weights(w_in, b_gates):
    w_rows = jnp.swapaxes(w_in, 1, 2).astype(_BF16)
    kinds = (0, 2, 1, 3)
    gate_rows = [w_rows[:, _OFF_G + kind * ML_HEADS:_OFF_G + (kind + 1) * ML_HEADS] for kind in kinds]
    w_kg = jnp.concatenate([w_rows[:, _OFF_K:_OFF_K + ML_W]] + gate_rows, axis=1)
    bias = jnp.concatenate([b_gates[:, kind * ML_HEADS:(kind + 1) * ML_HEADS] for kind in kinds], axis=-1)
    bias = jnp.broadcast_to(bias.astype(_F32)[:, :, None], bias.shape + (CHUNK,))
    return w_rows, w_kg, bias[:, :N_DIRHEAD], bias[:, N_DIRHEAD:]


def kernel(x, c, ctx, c_ctx, w_ada, b_ada, norm1, norm2, w_in, b_gates, ml_norm, conv_w, gm_norm, gm_ws,
           gm_bs, w_out, w1, w3, w2, norm_f):
    bsz, seq, _ = x.shape
    ctx_len = ctx.shape[1]
    depth = w_in.shape[0]
    total = seq + ctx_len
    assert bsz < MOD_ROWS and w_in.shape[-1] == _D_IN and w1.shape[-1] == D_FF
    assert seq % GRID_W == 0 and seq % CHUNK == 0 and ctx_len % CHUNK == 0
    assert total % TILE_ROWS == 0 and TILE_ROWS % CHUNK == 0 and total - TILE_ROWS <= seq
    assert seq % LAST_TILE_ROWS == 0 and TILE_ROWS % SUB_ROWS == 0 and LAST_TILE_ROWS % SUB_ROWS == 0
    assert seq % SUB_ROWS == 0 and ctx_len == SUB_ROWS and SUB_ROWS % CHUNK == 0 and SUB_ROWS % GRID_W == 0
    n_ctx_chunks = ctx_len // CHUNK

    cc = jnp.concatenate([c, c_ctx[None, :], jnp.zeros((MOD_ROWS - bsz - 1, D_MODEL), _F32)], axis=0)
    mods = _ada_table(cc, w_ada, b_ada)

    w_main, w_kg, bias_in, bias_fg = _prepare_in_weights(w_in, b_gates)
    wo_b, w1_b, w3_b, w2_b = (w.astype(_BF16) for w in (w_out, w1, w3, w2))
    gm_w_b = jnp.swapaxes(gm_ws, 1, 2).reshape(depth, CHUNK, GM_GROUPS * CHUNK).astype(_BF16)
    gm_bias = jnp.repeat(jnp.swapaxes(gm_bs, 1, 2), GM_W // GM_GROUPS, axis=2)
    conv_w8 = jnp.pad(conv_w, ((0, 0), (0, SUBLANES - conv_w.shape[1]), (0, 0)))
    norm1_r = norm1.reshape(depth, 1, D_MODEL)
    norm2_r = norm2.reshape(depth, 1, D_MODEL)
    gm_norm_r = gm_norm.reshape(depth, 1, GM_W)
    ml_norm_r = ml_norm.reshape(depth, 1, ML_W)
    norm_f_r = norm_f.reshape(1, D_MODEL)

    x_arr, ctx_arr, ctx_block = x, ctx, 0
    for l in range(depth):
        last = l == depth - 1
        p_ml, mix_loc, k_t, g_in, g_fg = _in_proj(x_arr, ctx_arr, mods, l, norm1_r, w_main, w_kg, conv_w8,
                                                  gm_norm_r, gm_w_b, gm_bias,
                                                  seq=seq, total=total, ctx_block=ctx_block)
        ml = _mlstm(p_ml, k_t, g_in, g_fg, bias_in[l], bias_fg[l], ml_norm_r[l], wo_b[l, :ML_W],
                    n_ctx_chunks=n_ctx_chunks, ctx_out=not last)
        xs = _out_ffn(x_arr, ctx_arr, ml, mix_loc, mods, l, norm2_r, wo_b[:, ML_W:], w1_b, w3_b, w2_b, norm_f_r,
                      seq=seq, ctx_block=ctx_block, out_rows=seq if last else total,
                      tile_rows=LAST_TILE_ROWS if last else TILE_ROWS, final_norm=last)
        x_arr, ctx_arr, ctx_block = xs, xs, seq // SUB_ROWS
    return xs
```
